```python
import math
import jax, jax.numpy as jnp
from jax import lax
import numpy as np

D_MODEL = 2048
BATCH = 4
SEQ = 8192
DEPTH = 4

N_MIXERS = 2
N_MLSTM = (DEPTH + 1) // 2
N_FOX = DEPTH // 2
MLSTM_HEADS = 4
MLSTM_DQK = D_MODEL // (2 * MLSTM_HEADS)
MLSTM_DV = D_MODEL // MLSTM_HEADS
MLSTM_QK = MLSTM_HEADS * MLSTM_DQK
MLSTM_V = MLSTM_HEADS * MLSTM_DV
MLSTM_IN = 2 * MLSTM_QK + 2 * MLSTM_V + 2 * MLSTM_HEADS
MLSTM_CHUNK = 64
GATE_SOFTCAP = 15.0
FOX_HEAD_DIM = 128
FOX_HEADS = D_MODEL // FOX_HEAD_DIM
FOX_IN = 3 * D_MODEL + FOX_HEADS
Q_BLOCK = 128
FFN_MULT = 256
D_FF = ((8 * D_MODEL + 3 * FFN_MULT - 1) // (3 * FFN_MULT)) * FFN_MULT
EPS = 1e-6

kernel_name = "hybrid_mlstm_fox_adaln_trunk"


def rms_norm(x, w):
    xf = x.astype(jnp.float32)
    y = xf * lax.rsqrt(jnp.mean(xf * xf, axis=-1, keepdims=True) + EPS)
    return (y * w.astype(jnp.float32)).astype(x.dtype)


def mlstm_chunkwise(q, k, v, log_i, log_f):
    B, H, T, dqk = q.shape
    dv = v.shape[-1]
    L = MLSTM_CHUNK
    nc = T // L

    def to_chunks(a):
        a = a.reshape((B, H, nc, L) + a.shape[3:])
        return jnp.moveaxis(a, 2, 0)

    causal = jnp.arange(L)[:, None] >= jnp.arange(L)[None, :]

    def step(carry, inp):
        C, n, m = carry
        qc, kc, vc, lic, lfc = inp
        b = jnp.cumsum(lfc, axis=-1)
        g = b[..., -1]
        D = b[..., :, None] - b[..., None, :] + lic[..., None, :]
        D = jnp.where(causal, D, -jnp.inf)
        inter_log = b + m[..., None]
        m_t = jnp.maximum(inter_log, jnp.max(D, axis=-1))
        w_intra = jnp.exp(D - m_t[..., None])
        w_inter = jnp.exp(inter_log - m_t)
        s = jnp.einsum('bhtd,bhsd->bhts', qc, kc) * w_intra
        num = w_inter[..., None] * jnp.einsum('bhtd,bhde->bhte', qc, C) + jnp.einsum('bhts,bhse->bhte', s, vc)
        den = w_inter * jnp.einsum('bhtd,bhd->bht', qc, n) + jnp.sum(s, axis=-1)
        h = num / jnp.maximum(jnp.abs(den), jnp.exp(-m_t))[..., None]
        tail = g[..., None] - b + lic
        m_new = jnp.maximum(g + m, jnp.max(tail, axis=-1))
        w_tail = jnp.exp(tail - m_new[..., None])
        decay = jnp.exp(g + m - m_new)
        C_new = decay[..., None, None] * C + jnp.einsum('bhs,bhsd,bhse->bhde', w_tail, kc, vc)
        n_new = decay[..., None] * n + jnp.einsum('bhs,bhsd->bhd', w_tail, kc)
        return (C_new, n_new, m_new), h

    init = (jnp.zeros((B, H, dqk, dv), jnp.float32),
            jnp.zeros((B, H, dqk), jnp.float32),
            jnp.zeros((B, H), jnp.float32))
    _, hs = lax.scan(step, init, (to_chunks(q), to_chunks(k), to_chunks(v), to_chunks(log_i), to_chunks(log_f)))
    return jnp.moveaxis(hs, 0, 2).reshape(B, H, T, dv)


def mlstm_mixer(h, w_in, b_gates, norm_w, w_out):
    B, T, _ = h.shape
    Hh = MLSTM_HEADS
    proj = h @ w_in
    q, k, v, o, gates = jnp.split(proj, [MLSTM_QK, 2 * MLSTM_QK, 2 * MLSTM_QK + MLSTM_V, 2 * MLSTM_QK + 2 * MLSTM_V], axis=-1)
    heads = lambda a, d: a.astype(jnp.float32).reshape(B, T, Hh, d).transpose(0, 2, 1, 3)
    q = heads(q, MLSTM_DQK) * (MLSTM_DQK ** -0.5)
    k = heads(k, MLSTM_DQK)
    v = heads(v, MLSTM_DV)
    gates = gates.astype(jnp.float32) + b_gates.astype(jnp.float32)
    gates = GATE_SOFTCAP * jnp.tanh(gates / GATE_SOFTCAP)
    log_i = gates[..., :Hh].transpose(0, 2, 1)
    log_f = jax.nn.log_sigmoid(gates[..., Hh:]).transpose(0, 2, 1)
    hs = mlstm_chunkwise(q, k, v, log_i, log_f)
    hs = hs * lax.rsqrt(jnp.mean(hs * hs, axis=-1, keepdims=True) + EPS)
    hs = hs.transpose(0, 2, 1, 3).reshape(B, T, MLSTM_V) * norm_w.astype(jnp.float32)
    y = jax.nn.sigmoid(o.astype(jnp.float32)) * hs
    return y.astype(h.dtype) @ w_out


def forgetting_attention(q, k, v, log_f):
    B, H, T, dh = q.shape
    nb = T // Q_BLOCK
    cum = jnp.cumsum(log_f, axis=-1)
    kf = k.astype(jnp.float32)
    vf = v.astype(jnp.float32)
    qb = jnp.moveaxis(q.astype(jnp.float32).reshape(B, H, nb, Q_BLOCK, dh), 2, 0)
    cqb = jnp.moveaxis(cum.reshape(B, H, nb, Q_BLOCK), 2, 0)
    kpos = jnp.arange(T)
    scale = dh ** -0.5

    def block(args):
        qi, cqi, bi = args
        s = jnp.einsum('bhqd,bhkd->bhqk', qi, kf) * scale + cqi[..., None] - cum[..., None, :]
        qpos = bi * Q_BLOCK + jnp.arange(Q_BLOCK)
        s = jnp.where(kpos[None, :] <= qpos[:, None], s, -jnp.inf)
        p = jax.nn.softmax(s, axis=-1)
        return jnp.einsum('bhqk,bhkd->bhqd', p, vf)

    out = lax.map(block, (qb, cqb, jnp.arange(nb)))
    return jnp.moveaxis(out, 0, 2).reshape(B, H, T, dh)


def fox_mixer(h, w_in, b_f, w_out):
    B, T, _ = h.shape
    Hh = FOX_HEADS
    proj = h @ w_in
    q, k, v, fl = jnp.split(proj, [D_MODEL, 2 * D_MODEL, 3 * D_MODEL], axis=-1)
    heads = lambda a: a.reshape(B, T, Hh, FOX_HEAD_DIM).transpose(0, 2, 1, 3)
    log_f = jax.nn.log_sigmoid(fl.astype(jnp.float32) + b_f.astype(jnp.float32)).transpose(0, 2, 1)
    o = forgetting_attention(heads(q), heads(k), heads(v), log_f)
    o = o.transpose(0, 2, 1, 3).reshape(B, T, D_MODEL).astype(h.dtype)
    return o @ w_out


def swiglu(h, w_gate_up, w_down):
    g, u = jnp.split(h @ w_gate_up, 2, axis=-1)
    return (jax.nn.silu(g) * u) @ w_down


def setup_inputs(seed: int = 0) -> dict:
    key = jax.random.key(seed)
    ks = jax.random.split(key, 20)
    f32 = jnp.float32
    nrm = lambda k, shape, s: jax.random.normal(k, shape, f32) * s
    x = nrm(ks[0], (BATCH, SEQ, D_MODEL), 1.0)
    c = nrm(ks[1], (BATCH, D_MODEL), 1.0)
    ada_w = nrm(ks[2], (DEPTH, D_MODEL, 6 * D_MODEL), 0.5 * D_MODEL ** -0.5)
    ada_b = nrm(ks[3], (DEPTH, 6 * D_MODEL), 0.02)
    norm_mix_w = 1.0 + nrm(ks[4], (DEPTH, D_MODEL), 0.05)
    norm_ffn_w = 1.0 + nrm(ks[5], (DEPTH, D_MODEL), 0.05)
    mlstm_w_in = nrm(ks[6], (N_MLSTM, D_MODEL, MLSTM_IN), D_MODEL ** -0.5)
    b_i = nrm(ks[7], (N_MLSTM, MLSTM_HEADS), 0.1)
    b_f = jnp.linspace(3.0, 6.0, MLSTM_HEADS, dtype=f32)[None, :] + nrm(ks[8], (N_MLSTM, MLSTM_HEADS), 0.1)
    mlstm_b_gates = jnp.concatenate([b_i, b_f], axis=-1)
    mlstm_norm_w = 1.0 + nrm(ks[9], (N_MLSTM, MLSTM_V), 0.05)
    mlstm_w_out = nrm(ks[10], (N_MLSTM, MLSTM_V, D_MODEL), MLSTM_V ** -0.5)
    fox_w_in = nrm(ks[11], (N_FOX, D_MODEL, FOX_IN), D_MODEL ** -0.5)
    fox_b_f = 4.0 + nrm(ks[12], (N_FOX, FOX_HEADS), 0.5)
    fox_w_out = nrm(ks[13], (N_FOX, D_MODEL, D_MODEL), D_MODEL ** -0.5)
    ffn_w_gate_up = nrm(ks[14], (DEPTH, D_MODEL, 2 * D_FF), D_MODEL ** -0.5)
    ffn_w_down = nrm(ks[15], (DEPTH, D_FF, D_MODEL), D_FF ** -0.5)
    final_norm_w = 1.0 + nrm(ks[16], (D_MODEL,), 0.05)
    return {"x": x, "c": c, "ada_w": ada_w, "ada_b": ada_b,
            "norm_mix_w": norm_mix_w, "norm_ffn_w": norm_ffn_w,
            "mlstm_w_in": mlstm_w_in, "mlstm_b_gates": mlstm_b_gates,
            "mlstm_norm_w": mlstm_norm_w, "mlstm_w_out": mlstm_w_out,
            "fox_w_in": fox_w_in, "fox_b_f": fox_b_f, "fox_w_out": fox_w_out,
            "ffn_w_gate_up": ffn_w_gate_up, "ffn_w_down": ffn_w_down,
            "final_norm_w": final_norm_w}


def reference(x, c, ada_w, ada_b, norm_mix_w, norm_ffn_w, mlstm_w_in, mlstm_b_gates,
              mlstm_norm_w, mlstm_w_out, fox_w_in, fox_b_f, fox_w_out,
              ffn_w_gate_up, ffn_w_down, final_norm_w):
    cs = jax.nn.silu(c)
    for layer in range(DEPTH):
        mod = cs @ ada_w[layer] + ada_b[layer]
        sh1, sc1, g1, sh2, sc2, g2 = [m[:, None, :] for m in jnp.split(mod, 6, axis=-1)]
        h = rms_norm(x, norm_mix_w[layer]) * (1.0 + sc1) + sh1
        j = layer // N_MIXERS
        if layer % N_MIXERS == 0:
            y = mlstm_mixer(h, mlstm_w_in[j], mlstm_b_gates[j], mlstm_norm_w[j], mlstm_w_out[j])
        else:
            y = fox_mixer(h, fox_w_in[j], fox_b_f[j], fox_w_out[j])
        x = x + g1 * y
        h = rms_norm(x, norm_ffn_w[layer]) * (1.0 + sc2) + sh2
        x = x + g2 * swiglu(h, ffn_w_gate_up[layer], ffn_w_down[layer])
    return rms_norm(x, final_norm_w)
```

```python
import functools

import jax
import jax.numpy as jnp
from jax import lax
from jax.experimental import pallas as pl
from jax.experimental.pallas import tpu as pltpu

F32 = jnp.float32
BF16 = jnp.bfloat16

EPS = 1e-6
GATE_SOFTCAP = 15.0
MLSTM_HEADS = 4
FOX_HEAD_DIM = 128
GATE_ROWS = 16
LANES = 128
NEG_BIG = -1e30

VMEM_LIMIT = 48 * 1024 * 1024


def _params(*sem):
    return pltpu.CompilerParams(dimension_semantics=sem, vmem_limit_bytes=VMEM_LIMIT)


def _tile(n, pref):
    t = min(n, pref)
    assert n % t == 0, (n, pref)
    return t


def _log_sigmoid(x):
    return jnp.minimum(x, 0.0) - jnp.log1p(jnp.exp(-jnp.abs(x)))


def _sigmoid(x):
    return 1.0 / (1.0 + jnp.exp(-x))


def _dot(a, b):
    return jnp.dot(a, b, preferred_element_type=F32)


def _dot_nt(a, b):
    return lax.dot_general(a, b, (((1,), (1,)), ((), ())), preferred_element_type=F32)


def _dot_tn(a, b):
    return lax.dot_general(a, b, (((0,), (0,)), ((), ())), preferred_element_type=F32)


def _modulated_norm(x, nw, sc, sh):
    y = x * lax.rsqrt(jnp.mean(x * x, axis=-1, keepdims=True) + EPS)
    return (y * nw) * (1.0 + sc) + sh


def _mod_kernel(c_ref, w_ref, b_ref, o_ref):
    c = c_ref[...]
    cs = (c * _sigmoid(c)).astype(BF16)
    o_ref[0] = _dot(cs, w_ref[0].astype(BF16)) + b_ref[0]


def _modulation(c_pad, ada_w, ada_b):
    depth, d, n6 = ada_w.shape
    rows = c_pad.shape[0]
    tn = _tile(n6, 1024)
    return pl.pallas_call(
        _mod_kernel,
        out_shape=jax.ShapeDtypeStruct((depth, rows, n6), F32),
        grid=(depth, n6 // tn),
        in_specs=[pl.BlockSpec((rows, d), lambda l, j: (0, 0)),
                  pl.BlockSpec((1, d, tn), lambda l, j: (l, 0, j)),
                  pl.BlockSpec((1, 1, tn), lambda l, j: (l, 0, j))],
        out_specs=pl.BlockSpec((1, rows, tn), lambda l, j: (l, 0, j)),
        compiler_params=_params("parallel", "parallel"),
        name="adaln_modulation",
    )(c_pad, ada_w, ada_b.reshape(depth, 1, n6))


def _inproj_kernel(x_ref, nw_ref, sc_ref, sh_ref, w_ref, wg_ref, o_ref, g_ref, h_ref, *, q_tiles, q_scale):
    j = pl.program_id(1)

    @pl.when(j == 0)
    def _():
        hb = _modulated_norm(x_ref[...], nw_ref[...], sc_ref[0], sh_ref[0]).astype(BF16)
        h_ref[...] = hb
        g_ref[0] = _dot_nt(wg_ref[...], hb)

    acc = _dot(h_ref[...], w_ref[...])
    scale = jnp.where(j < q_tiles, q_scale, 1.0).astype(F32)
    o_ref[...] = (acc * scale).astype(o_ref.dtype)


def _inproj(x, nw, sc, sh, w, wg_t, *, batch, q_cols, q_scale):
    n, d = x.shape
    t = n // batch
    nout = w.shape[1]
    tm = _tile(t, 1024)
    tn = _tile(nout, 1024)
    assert q_cols % tn == 0
    tpb = t // tm
    kern = functools.partial(_inproj_kernel, q_tiles=q_cols // tn, q_scale=q_scale)
    return pl.pallas_call(
        kern,
        out_shape=(jax.ShapeDtypeStruct((n, nout), BF16),
                   jax.ShapeDtypeStruct((batch, GATE_ROWS, t), F32)),
        grid=(n // tm, nout // tn),
        in_specs=[pl.BlockSpec((tm, d), lambda i, j: (i, 0)),
                  pl.BlockSpec((1, d), lambda i, j: (0, 0)),
                  pl.BlockSpec((1, 1, d), lambda i, j: (i // tpb, 0, 0)),
                  pl.BlockSpec((1, 1, d), lambda i, j: (i // tpb, 0, 0)),
                  pl.BlockSpec((d, tn), lambda i, j: (0, j)),
                  pl.BlockSpec((GATE_ROWS, d), lambda i, j: (0, 0))],
        out_specs=(pl.BlockSpec((tm, tn), lambda i, j: (i, j)),
                   pl.BlockSpec((1, GATE_ROWS, tm), lambda i, j: (i // tpb, 0, i % tpb))),
        scratch_shapes=[pltpu.VMEM((tm, d), BF16)],
        compiler_params=_params("parallel", "arbitrary"),
        name="norm_inproj",
    )(x, nw, sc, sh, w, wg_t)


def _outproj_kernel(y_ref, w_ref, x_ref, g_ref, o_ref):
    o_ref[...] = x_ref[...] + g_ref[0] * _dot(y_ref[...], w_ref[...])


def _outproj(y, w, x, gate, *, batch):
    n, kdim = y.shape
    d = w.shape[1]
    t = n // batch
    tm = _tile(t, 1024)
    tn = _tile(d, 1024)
    tpb = t // tm
    return pl.pallas_call(
        _outproj_kernel,
        out_shape=jax.ShapeDtypeStruct((n, d), F32),
        grid=(n // tm, d // tn),
        in_specs=[pl.BlockSpec((tm, kdim), lambda i, j: (i, 0)),
                  pl.BlockSpec((kdim, tn), lambda i, j: (0, j)),
                  pl.BlockSpec((tm, tn), lambda i, j: (i, j)),
                  pl.BlockSpec((1, 1, tn), lambda i, j: (i // tpb, 0, j))],
        out_specs=pl.BlockSpec((tm, tn), lambda i, j: (i, j)),
        compiler_params=_params("parallel", "arbitrary"),
        name="outproj_residual",
    )(y, w, x, gate)


def _ffn_kernel(x_ref, nw_ref, sc_ref, sh_ref, g2_ref, wg_ref, wu_ref, wd_ref, fw_ref, o_ref, h_ref, acc_ref,
                *, final_norm):
    f = pl.program_id(1)

    @pl.when(f == 0)
    def _():
        h_ref[...] = _modulated_norm(x_ref[...], nw_ref[...], sc_ref[0], sh_ref[0]).astype(BF16)
        acc_ref[...] = jnp.zeros_like(acc_ref)

    h = h_ref[...]
    g = _dot(h, wg_ref[...])
    u = _dot(h, wu_ref[...])
    a = (g * _sigmoid(g) * u).astype(BF16)
    acc_ref[...] += _dot(a, wd_ref[...])

    @pl.when(f == pl.num_programs(1) - 1)
    def _():
        out = x_ref[...] + g2_ref[0] * acc_ref[...]
        if final_norm:
            out = out * lax.rsqrt(jnp.mean(out * out, axis=-1, keepdims=True) + EPS) * fw_ref[...]
        o_ref[...] = out


def _ffn(x, nw, sc, sh, g2, w_gate_up, w_down, final_w, *, batch, final_norm):
    n, d = x.shape
    dff = w_down.shape[0]
    t = n // batch
    tm = _tile(t, 512)
    tf = _tile(dff, 512)
    tpb = t // tm
    nf = dff // tf
    vec = lambda: pl.BlockSpec((1, 1, d), lambda i, f: (i // tpb, 0, 0))
    return pl.pallas_call(
        functools.partial(_ffn_kernel, final_norm=final_norm),
        out_shape=jax.ShapeDtypeStruct((n, d), F32),
        grid=(n // tm, nf),
        in_specs=[pl.BlockSpec((tm, d), lambda i, f: (i, 0)),
                  pl.BlockSpec((1, d), lambda i, f: (0, 0)),
                  vec(), vec(), vec(),
                  pl.BlockSpec((d, tf), lambda i, f: (0, f)),
                  pl.BlockSpec((d, tf), lambda i, f: (0, nf + f)),
                  pl.BlockSpec((tf, d), lambda i, f: (f, 0)),
                  pl.BlockSpec((1, d), lambda i, f: (0, 0))],
        out_specs=pl.BlockSpec((tm, d), lambda i, f: (i, 0)),
        scratch_shapes=[pltpu.VMEM((tm, d), BF16), pltpu.VMEM((tm, d), F32)],
        compiler_params=_params("parallel", "arbitrary"),
        name="swiglu_ffn",
    )(x, nw, sc, sh, g2, w_gate_up, w_gate_up, w_down, final_w)


def _mlstm_kernel(q_ref, k_ref, v_ref, o_ref, g_ref, bias_ref, nw_ref, y_ref, c_ref, n_ref, m_ref, *, heads):
    hh = pl.program_id(1)
    ci = pl.program_id(2)
    L = q_ref.shape[0]

    @pl.when(ci == 0)
    def _():
        c_ref[...] = jnp.zeros_like(c_ref)
        n_ref[...] = jnp.zeros_like(n_ref)
        m_ref[...] = jnp.zeros_like(m_ref)

    def gate_row(r):
        z = g_ref[0, pl.ds(r, 1), :] + bias_ref[pl.ds(r, 1), :]
        return GATE_SOFTCAP * jnp.tanh(z / GATE_SOFTCAP)

    li_row = gate_row(hh)
    lf_row = _log_sigmoid(gate_row(heads + hh))

    row = lax.broadcasted_iota(jnp.int32, (L, L), 0)
    col = lax.broadcasted_iota(jnp.int32, (L, L), 1)
    causal = row >= col
    eye = row == col
    b_col = jnp.sum(jnp.where(causal, lf_row, 0.0), axis=1, keepdims=True)
    b_row = jnp.sum(jnp.where(eye, b_col, 0.0), axis=0, keepdims=True)
    li_col = jnp.sum(jnp.where(eye, li_row, 0.0), axis=1, keepdims=True)
    g_tot = jnp.sum(lf_row, axis=1, keepdims=True)

    m_prev = m_ref[0:1, 0:1]
    dmat = jnp.where(causal, b_col + (li_row - b_row), -jnp.inf)
    inter_log = b_col + m_prev
    m_t = jnp.maximum(inter_log, jnp.max(dmat, axis=1, keepdims=True))
    w_intra = jnp.exp(dmat - m_t)
    w_inter = jnp.exp(inter_log - m_t)

    q = q_ref[...]
    k = k_ref[...]
    v = v_ref[...]
    ones = jnp.ones((L, LANES), BF16)
    s = (_dot_nt(q, k) * w_intra).astype(BF16)
    cb = c_ref[...].astype(BF16)
    nb = n_ref[...].astype(BF16)
    num = w_inter * _dot(q, cb) + _dot(s, v)
    den = w_inter * _dot(q, nb)[:, 0:1] + _dot(s, ones)[:, 0:1]
    h = num * (1.0 / jnp.maximum(jnp.abs(den), jnp.exp(-m_t)))

    tail = g_tot - b_col + li_col
    m_new = jnp.maximum(g_tot + m_prev, jnp.max(tail, axis=0, keepdims=True))
    w_tail = jnp.exp(tail - m_new)
    decay = jnp.exp(g_tot + m_prev - m_new)
    kw = (k.astype(F32) * w_tail).astype(BF16)
    c_ref[...] = decay * c_ref[...] + _dot_tn(kw, v)
    n_ref[...] = decay * n_ref[...] + _dot_tn(kw, ones)
    m_ref[...] = jnp.broadcast_to(m_new, m_ref.shape)

    hs = h * lax.rsqrt(jnp.mean(h * h, axis=-1, keepdims=True) + EPS) * nw_ref[...]
    y_ref[...] = (_sigmoid(o_ref[...].astype(F32)) * hs).astype(y_ref.dtype)


def _mlstm(proj, gates_t, bias, norm_w, *, batch, d_model):
    n = proj.shape[0]
    t = n // batch
    heads = MLSTM_HEADS
    dqk = d_model // (2 * heads)
    dv = d_model // heads
    L = _tile(t, 256)
    nc = t // L
    k_off = heads
    v_off = (2 * heads * dqk) // dv
    o_off = v_off + heads
    return pl.pallas_call(
        functools.partial(_mlstm_kernel, heads=heads),
        out_shape=jax.ShapeDtypeStruct((n, heads * dv), BF16),
        grid=(batch, heads, nc),
        in_specs=[pl.BlockSpec((L, dqk), lambda b, h, c: (b * nc + c, h)),
                  pl.BlockSpec((L, dqk), lambda b, h, c: (b * nc + c, k_off + h)),
                  pl.BlockSpec((L, dv), lambda b, h, c: (b * nc + c, v_off + h)),
                  pl.BlockSpec((L, dv), lambda b, h, c: (b * nc + c, o_off + h)),
                  pl.BlockSpec((1, GATE_ROWS, L), lambda b, h, c: (b, 0, c)),
                  pl.BlockSpec((GATE_ROWS, 1), lambda b, h, c: (0, 0)),
                  pl.BlockSpec((1, dv), lambda b, h, c: (0, h))],
        out_specs=pl.BlockSpec((L, dv), lambda b, h, c: (b * nc + c, h)),
        scratch_shapes=[pltpu.VMEM((dqk, dv), F32), pltpu.VMEM((dqk, LANES), F32), pltpu.VMEM((8, LANES), F32)],
        compiler_params=_params("parallel", "parallel", "arbitrary"),
        name="mlstm_chunkwise",
    )(proj, proj, proj, proj, gates_t, bias, norm_w)


def _cumgate_kernel(g_ref, bias_ref, o_ref, carry_ref):
    @pl.when(pl.program_id(1) == 0)
    def _():
        carry_ref[...] = jnp.zeros_like(carry_ref)

    lf = _log_sigmoid(g_ref[0] + bias_ref[...])
    tc = lf.shape[1]
    row = lax.broadcasted_iota(jnp.int32, (tc, tc), 0)
    col = lax.broadcasted_iota(jnp.int32, (tc, tc), 1)
    upper = jnp.where(row <= col, 1.0, 0.0).astype(BF16)
    p0 = lf.astype(BF16)
    r1 = lf - p0.astype(F32)
    p1 = r1.astype(BF16)
    p2 = (r1 - p1.astype(F32)).astype(BF16)
    cum = _dot(p0, upper) + _dot(p1, upper) + _dot(p2, upper) + carry_ref[:, 0:1]
    o_ref[0] = cum
    carry_ref[...] = jnp.broadcast_to(cum[:, tc - 1:tc], carry_ref.shape)


def _cumgate(gates_t, bias):
    batch, rows, t = gates_t.shape
    tc = _tile(t, 512)
    return pl.pallas_call(
        _cumgate_kernel,
        out_shape=jax.ShapeDtypeStruct((batch, rows, t), F32),
        grid=(batch, t // tc),
        in_specs=[pl.BlockSpec((1, rows, tc), lambda b, c: (b, 0, c)),
                  pl.BlockSpec((rows, 1), lambda b, c: (0, 0))],
        out_specs=pl.BlockSpec((1, rows, tc), lambda b, c: (b, 0, c)),
        scratch_shapes=[pltpu.VMEM((rows, LANES), F32)],
        compiler_params=_params("parallel", "arbitrary"),
        name="fox_cum_log_forget",
    )(gates_t, bias)


def _fox_kernel(q_ref, k_ref, v_ref, ck_ref, o_ref, *, tk):
    hh = pl.program_id(1)
    qi = pl.program_id(2)
    tq, dh = q_ref.shape
    assert tq == tk
    q = q_ref[...]

    def block(ki, carry, masked):
        m, l, acc = carry
        start = pl.multiple_of(ki * tk, tk)
        k = k_ref[pl.ds(start, tk), :]
        v = v_ref[pl.ds(start, tk), :]
        ck = ck_ref[0, pl.ds(hh, 1), pl.ds(start, tk)]
        s = _dot_nt(q, k) - ck
        if masked:
            row = lax.broadcasted_iota(jnp.int32, (tq, tk), 0)
            col = lax.broadcasted_iota(jnp.int32, (tq, tk), 1)
            s = jnp.where(col <= row, s, -jnp.inf)
        m_new = jnp.maximum(m, jnp.max(s, axis=1, keepdims=True))
        alpha = jnp.exp(m - m_new)
        p = jnp.exp(s - m_new)
        l = alpha * l + jnp.sum(p, axis=1, keepdims=True)
        acc = alpha * acc + _dot(p.astype(BF16), v)
        return m_new, l, acc

    init = (jnp.full((tq, 1), NEG_BIG, F32), jnp.zeros((tq, 1), F32), jnp.zeros((tq, dh), F32))
    carry = lax.fori_loop(0, qi, lambda ki, c: block(ki, c, False), init)
    m, l, acc = block(qi, carry, True)
    o_ref[...] = (acc * (1.0 / l)).astype(o_ref.dtype)


def _fox_attention(proj, cum_t, *, batch, d_model):
    n = proj.shape[0]
    t = n // batch
    dh = FOX_HEAD_DIM
    heads = d_model // dh
    tq = _tile(t, 512)
    nq = t // tq
    rows = cum_t.shape[1]
    return pl.pallas_call(
        functools.partial(_fox_kernel, tk=tq),
        out_shape=jax.ShapeDtypeStruct((n, d_model), BF16),
        grid=(batch, heads, nq),
        in_specs=[pl.BlockSpec((tq, dh), lambda b, h, i: (b * nq + i, h)),
                  pl.BlockSpec((t, dh), lambda b, h, i: (b, heads + h)),
                  pl.BlockSpec((t, dh), lambda b, h, i: (b, 2 * heads + h)),
                  pl.BlockSpec((1, rows, t), lambda b, h, i: (b, 0, 0))],
        out_specs=pl.BlockSpec((tq, dh), lambda b, h, i: (b * nq + i, h)),
        compiler_params=_params("parallel", "parallel", "arbitrary"),
        name="fox_attention",
    )(proj, proj, proj, cum_t)


def _gate_weights_t(w_cols):
    g = w_cols.shape[1]
    return jnp.pad(w_cols.T, ((0, GATE_ROWS - g), (0, 0))).astype(BF16)


def _gate_bias(b):
    return jnp.pad(b.astype(F32), (0, GATE_ROWS - b.shape[0])).reshape(GATE_ROWS, 1)


def kernel(x, c, ada_w, ada_b, norm_mix_w, norm_ffn_w, mlstm_w_in, mlstm_b_gates, mlstm_norm_w, mlstm_w_out,
           fox_w_in, fox_b_f, fox_w_out, ffn_w_gate_up, ffn_w_down, final_norm_w):
    batch, t, d = x.shape
    depth = ada_w.shape[0]
    n = batch * t
    dqk = d // (2 * MLSTM_HEADS)
    mlstm_main = 2 * MLSTM_HEADS * dqk + 2 * d
    fox_main = 3 * d

    c_pad = jnp.pad(c, ((0, GATE_ROWS - batch), (0, 0)))
    mod = _modulation(c_pad, ada_w, ada_b)[:, :batch]
    mod = mod.reshape(depth, batch, 6, 1, d)

    xf = x.reshape(n, d)
    final_w = final_norm_w.reshape(1, d)
    for layer in range(depth):
        sh1, sc1, g1, sh2, sc2, g2 = [mod[layer, :, i] for i in range(6)]
        j = layer // 2
        nw = norm_mix_w[layer].reshape(1, d)
        if layer % 2 == 0:
            w_in = mlstm_w_in[j]
            proj, gates_t = _inproj(xf, nw, sc1, sh1, w_in[:, :mlstm_main].astype(BF16),
                                    _gate_weights_t(w_in[:, mlstm_main:]),
                                    batch=batch, q_cols=MLSTM_HEADS * dqk, q_scale=dqk ** -0.5)
            y = _mlstm(proj, gates_t, _gate_bias(mlstm_b_gates[j]), mlstm_norm_w[j].reshape(1, d),
                       batch=batch, d_model=d)
            w_out = mlstm_w_out[j]
        else:
            w_in = fox_w_in[j]
            proj, gates_t = _inproj(xf, nw, sc1, sh1, w_in[:, :fox_main].astype(BF16),
                                    _gate_weights_t(w_in[:, fox_main:]),
                                    batch=batch, q_cols=d, q_scale=FOX_HEAD_DIM ** -0.5)
            cum_t = _cumgate(gates_t, _gate_bias(fox_b_f[j]))
            y = _fox_attention(proj, cum_t, batch=batch, d_model=d)
            w_out = fox_w_out[j]
        xf = _outproj(y, w_out.astype(BF16), xf, g1, batch=batch)
        xf = _ffn(xf, norm_ffn_w[layer].reshape(1, d), sc2, sh2, g2,
                  ffn_w_gate_up[layer].astype(BF16), ffn_w_down[layer].astype(BF16), final_w,
                  batch=batch, final_norm=(layer == depth - 1))
    return xf.reshape(batch, t, d)
```

```python
import functools

import jax
import jax.numpy as jnp
from jax import lax
from jax.experimental import pallas as pl
from jax.experimental.pallas import tpu as pltpu

F32 = jnp.float32
BF16 = jnp.bfloat16

EPS = 1e-6
GATE_SOFTCAP = 15.0
MLSTM_HEADS = 4
FOX_HEAD_DIM = 128
GATE_ROWS = 16
LANES = 128
NEG_BIG = -1e30
LOG2E = 1.4426950408889634

VMEM_LIMIT = 56 * 1024 * 1024


def _params(*sem):
    return pltpu.CompilerParams(dimension_semantics=sem, vmem_limit_bytes=VMEM_LIMIT)


def _tile(n, pref):
    t = min(n, pref)
    assert n % t == 0, (n, pref)
    return t


def _log_sigmoid(x):
    return jnp.minimum(x, 0.0) - jnp.log1p(jnp.exp(-jnp.abs(x)))


def _sigmoid(x):
    return 1.0 / (1.0 + jnp.exp(-x))


def _dot(a, b):
    return jnp.dot(a, b, preferred_element_type=F32)


def _dot_nt(a, b):
    return lax.dot_general(a, b, (((1,), (1,)), ((), ())), preferred_element_type=F32)


def _dot_tn(a, b):
    return lax.dot_general(a, b, (((0,), (0,)), ((), ())), preferred_element_type=F32)


def _modulated_norm(x, nw, sc, sh):
    y = x * lax.rsqrt(jnp.mean(x * x, axis=-1, keepdims=True) + EPS)
    return (y * nw) * (1.0 + sc) + sh


def _mod_kernel(c_ref, w_ref, b_ref, o_ref):
    c = c_ref[...]
    cs = (c * _sigmoid(c)).astype(BF16)
    o_ref[0] = _dot(cs, w_ref[0].astype(BF16)) + b_ref[0]


def _modulation(c_pad, ada_w, ada_b):
    depth, d, n6 = ada_w.shape
    rows = c_pad.shape[0]
    tn = _tile(n6, 1024)
    return pl.pallas_call(
        _mod_kernel,
        out_shape=jax.ShapeDtypeStruct((depth, rows, n6), F32),
        grid=(depth, n6 // tn),
        in_specs=[pl.BlockSpec((rows, d), lambda l, j: (0, 0)),
                  pl.BlockSpec((1, d, tn), lambda l, j: (l, 0, j)),
                  pl.BlockSpec((1, 1, tn), lambda l, j: (l, 0, j))],
        out_specs=pl.BlockSpec((1, rows, tn), lambda l, j: (l, 0, j)),
        compiler_params=_params("parallel", "parallel"),
        name="adaln_modulation",
    )(c_pad, ada_w, ada_b.reshape(depth, 1, n6))


def _inproj_kernel(x_ref, nw_ref, sc_ref, sh_ref, w_ref, wg_ref, o_ref, g_ref, h_ref, *, q_tiles, q_scale,
                   gates_head_major):
    j = pl.program_id(1)

    @pl.when(j == 0)
    def _():
        hb = _modulated_norm(x_ref[...], nw_ref[...], sc_ref[0], sh_ref[0]).astype(BF16)
        h_ref[...] = hb
        if gates_head_major:
            g_ref[0] = _dot_nt(wg_ref[...], hb)
        else:
            g_ref[...] = _dot(hb, wg_ref[...])

    acc = _dot(h_ref[...], w_ref[...])
    scale = jnp.where(j < q_tiles, q_scale, 1.0).astype(F32)
    o_ref[...] = (acc * scale).astype(o_ref.dtype)


def _inproj(x, nw, sc, sh, w, wg, *, batch, q_cols, q_scale, gates_head_major):
    n, d = x.shape
    t = n // batch
    nout = w.shape[1]
    tm = _tile(t, 1024)
    tn = _tile(nout, 1024)
    assert q_cols % tn == 0
    tpb = t // tm
    kern = functools.partial(_inproj_kernel, q_tiles=q_cols // tn, q_scale=q_scale,
                             gates_head_major=gates_head_major)
    if gates_head_major:
        g_shape = jax.ShapeDtypeStruct((batch, GATE_ROWS, t), F32)
        g_spec = pl.BlockSpec((1, GATE_ROWS, tm), lambda i, j: (i // tpb, 0, i % tpb))
    else:
        g_shape = jax.ShapeDtypeStruct((n, LANES), F32)
        g_spec = pl.BlockSpec((tm, LANES), lambda i, j: (i, 0))
    return pl.pallas_call(
        kern,
        out_shape=(jax.ShapeDtypeStruct((n, nout), BF16), g_shape),
        grid=(n // tm, nout // tn),
        in_specs=[pl.BlockSpec((tm, d), lambda i, j: (i, 0)),
                  pl.BlockSpec((1, d), lambda i, j: (0, 0)),
                  pl.BlockSpec((1, 1, d), lambda i, j: (i // tpb, 0, 0)),
                  pl.BlockSpec((1, 1, d), lambda i, j: (i // tpb, 0, 0)),
                  pl.BlockSpec((d, tn), lambda i, j: (0, j)),
                  pl.BlockSpec(wg.shape, lambda i, j: (0, 0))],
        out_specs=(pl.BlockSpec((tm, tn), lambda i, j: (i, j)), g_spec),
        scratch_shapes=[pltpu.VMEM((tm, d), BF16)],
        compiler_params=_params("parallel", "arbitrary"),
        name="norm_inproj",
    )(x, nw, sc, sh, w, wg)


def _outproj_kernel(y_ref, w_ref, x_ref, g_ref, o_ref):
    o_ref[...] = x_ref[...] + g_ref[0] * _dot(y_ref[...], w_ref[...])


def _outproj(y, w, x, gate, *, batch):
    n, kdim = y.shape
    d = w.shape[1]
    t = n // batch
    tm = _tile(t, 1024)
    tn = _tile(d, 1024)
    tpb = t // tm
    return pl.pallas_call(
        _outproj_kernel,
        out_shape=jax.ShapeDtypeStruct((n, d), F32),
        grid=(n // tm, d // tn),
        in_specs=[pl.BlockSpec((tm, kdim), lambda i, j: (i, 0)),
                  pl.BlockSpec((kdim, tn), lambda i, j: (0, j)),
                  pl.BlockSpec((tm, tn), lambda i, j: (i, j)),
                  pl.BlockSpec((1, 1, tn), lambda i, j: (i // tpb, 0, j))],
        out_specs=pl.BlockSpec((tm, tn), lambda i, j: (i, j)),
        compiler_params=_params("parallel", "arbitrary"),
        name="outproj_residual",
    )(y, w, x, gate)


def _ffn_kernel(x_ref, nw_ref, sc_ref, sh_ref, g2_ref, wg_ref, wu_ref, wd_ref, fw_ref, o_ref, h_ref, *, final_norm):
    f = pl.program_id(1)

    @pl.when(f == 0)
    def _():
        x = x_ref[...]
        h_ref[...] = _modulated_norm(x, nw_ref[...], sc_ref[0], sh_ref[0]).astype(BF16)
        o_ref[...] = x

    h = h_ref[...]
    g = _dot(h, wg_ref[...])
    u = _dot(h, wu_ref[...])
    a = (g * _sigmoid(g) * u).astype(BF16)
    o_ref[...] += g2_ref[0] * _dot(a, wd_ref[...])

    if final_norm:
        @pl.when(f == pl.num_programs(1) - 1)
        def _():
            out = o_ref[...]
            o_ref[...] = out * lax.rsqrt(jnp.mean(out * out, axis=-1, keepdims=True) + EPS) * fw_ref[...]


def _ffn(x, nw, sc, sh, g2, w_gate_up, w_down, final_w, *, batch, final_norm):
    n, d = x.shape
    dff = w_down.shape[0]
    t = n // batch
    tm = _tile(t, 1024)
    tf = _tile(dff, 256)
    tpb = t // tm
    nf = dff // tf
    vec = lambda: pl.BlockSpec((1, 1, d), lambda i, f: (i // tpb, 0, 0))
    return pl.pallas_call(
        functools.partial(_ffn_kernel, final_norm=final_norm),
        out_shape=jax.ShapeDtypeStruct((n, d), F32),
        grid=(n // tm, nf),
        in_specs=[pl.BlockSpec((tm, d), lambda i, f: (i, 0)),
                  pl.BlockSpec((1, d), lambda i, f: (0, 0)),
                  vec(), vec(), vec(),
                  pl.BlockSpec((d, tf), lambda i, f: (0, f)),
                  pl.BlockSpec((d, tf), lambda i, f: (0, nf + f)),
                  pl.BlockSpec((tf, d), lambda i, f: (f, 0)),
                  pl.BlockSpec((1, d), lambda i, f: (0, 0))],
        out_specs=pl.BlockSpec((tm, d), lambda i, f: (i, 0)),
        scratch_shapes=[pltpu.VMEM((tm, d), BF16)],
        compiler_params=_params("parallel", "arbitrary"),
        name="swiglu_ffn",
    )(x, nw, sc, sh, g2, w_gate_up, w_gate_up, w_down, final_w)


def _mlstm_kernel(q_ref, k_ref, v_ref, o_ref, g_ref, bias_ref, nw_ref, y_ref, c_ref, n_ref, m_ref, *, heads):
    hh = pl.program_id(1)
    ci = pl.program_id(2)
    L = q_ref.shape[0]

    @pl.when(ci == 0)
    def _():
        c_ref[...] = jnp.zeros_like(c_ref)
        n_ref[...] = jnp.zeros_like(n_ref)
        m_ref[...] = jnp.zeros_like(m_ref)

    def gate_row(r):
        z = g_ref[0, pl.ds(r, 1), :] + bias_ref[pl.ds(r, 1), :]
        return GATE_SOFTCAP * jnp.tanh(z / GATE_SOFTCAP)

    li_row = gate_row(hh)
    lf_row = _log_sigmoid(gate_row(heads + hh))

    row = lax.broadcasted_iota(jnp.int32, (L, L), 0)
    col = lax.broadcasted_iota(jnp.int32, (L, L), 1)
    causal = row >= col
    eye = row == col
    b_col = jnp.sum(jnp.where(causal, lf_row, 0.0), axis=1, keepdims=True)
    b_row = jnp.sum(jnp.where(eye, b_col, 0.0), axis=0, keepdims=True)
    li_col = jnp.sum(jnp.where(eye, li_row, 0.0), axis=1, keepdims=True)
    g_tot = jnp.sum(lf_row, axis=1, keepdims=True)

    m_prev = m_ref[0:1, 0:1]
    dmat = jnp.where(causal, b_col + (li_row - b_row), -jnp.inf)
    inter_log = b_col + m_prev
    m_t = jnp.maximum(inter_log, jnp.max(dmat, axis=1, keepdims=True))
    w_intra = jnp.exp(dmat - m_t)
    w_inter = jnp.exp(inter_log - m_t)

    q = q_ref[...]
    k = k_ref[...]
    v = v_ref[...]
    ones = jnp.ones((L, LANES), BF16)
    s = (_dot_nt(q, k) * w_intra).astype(BF16)
    cb = c_ref[...].astype(BF16)
    nb = n_ref[...].astype(BF16)
    num = w_inter * _dot(q, cb) + _dot(s, v)
    den = w_inter * _dot(q, nb)[:, 0:1] + _dot(s, ones)[:, 0:1]
    h = num * (1.0 / jnp.maximum(jnp.abs(den), jnp.exp(-m_t)))

    tail = g_tot - b_col + li_col
    m_new = jnp.maximum(g_tot + m_prev, jnp.max(tail, axis=0, keepdims=True))
    w_tail = jnp.exp(tail - m_new)
    decay = jnp.exp(g_tot + m_prev - m_new)
    kw = (k.astype(F32) * w_tail).astype(BF16)
    c_ref[...] = decay * c_ref[...] + _dot_tn(kw, v)
    n_ref[...] = decay * n_ref[...] + _dot_tn(kw, ones)
    m_ref[...] = jnp.broadcast_to(m_new, m_ref.shape)

    hs = h * lax.rsqrt(jnp.mean(h * h, axis=-1, keepdims=True) + EPS) * nw_ref[...]
    y_ref[...] = (_sigmoid(o_ref[...].astype(F32)) * hs).astype(y_ref.dtype)


def _mlstm(proj, gates_t, bias, norm_w, *, batch, d_model):
    n = proj.shape[0]
    t = n // batch
    heads = MLSTM_HEADS
    dqk = d_model // (2 * heads)
    dv = d_model // heads
    L = _tile(t, 256)
    nc = t // L
    k_off = heads
    v_off = (2 * heads * dqk) // dv
    o_off = v_off + heads
    return pl.pallas_call(
        functools.partial(_mlstm_kernel, heads=heads),
        out_shape=jax.ShapeDtypeStruct((n, heads * dv), BF16),
        grid=(batch, heads, nc),
        in_specs=[pl.BlockSpec((L, dqk), lambda b, h, c: (b * nc + c, h)),
                  pl.BlockSpec((L, dqk), lambda b, h, c: (b * nc + c, k_off + h)),
                  pl.BlockSpec((L, dv), lambda b, h, c: (b * nc + c, v_off + h)),
                  pl.BlockSpec((L, dv), lambda b, h, c: (b * nc + c, o_off + h)),
                  pl.BlockSpec((1, GATE_ROWS, L), lambda b, h, c: (b, 0, c)),
                  pl.BlockSpec((GATE_ROWS, 1), lambda b, h, c: (0, 0)),
                  pl.BlockSpec((1, dv), lambda b, h, c: (0, h))],
        out_specs=pl.BlockSpec((L, dv), lambda b, h, c: (b * nc + c, h)),
        scratch_shapes=[pltpu.VMEM((dqk, dv), F32), pltpu.VMEM((dqk, LANES), F32), pltpu.VMEM((8, LANES), F32)],
        compiler_params=_params("parallel", "parallel", "arbitrary"),
        name="mlstm_chunkwise",
    )(proj, proj, proj, proj, gates_t, bias, norm_w)


BIAS_PIECES = 3


def _split_bf16(x):
    p0 = x.astype(BF16)
    r1 = x - p0.astype(F32)
    p1 = r1.astype(BF16)
    p2 = (r1 - p1.astype(F32)).astype(BF16)
    return p0, p1, p2


def _cumgate_kernel(g_ref, bias_ref, o_ref, carry_ref, *, heads):
    @pl.when(pl.program_id(1) == 0)
    def _():
        carry_ref[...] = jnp.zeros_like(carry_ref)

    lf = _log_sigmoid(g_ref[...] + bias_ref[...])
    tc = lf.shape[0]
    row = lax.broadcasted_iota(jnp.int32, (tc, tc), 0)
    col = lax.broadcasted_iota(jnp.int32, (tc, tc), 1)
    lower = jnp.where(row >= col, 1.0, 0.0).astype(BF16)
    cum = carry_ref[0:1, :]
    for piece in _split_bf16(lf):
        cum = cum + _dot(lower, piece)
    carry_ref[...] = jnp.broadcast_to(cum[tc - 1:tc, :], carry_ref.shape)

    width = heads * LANES
    src = lax.broadcasted_iota(jnp.int32, (LANES, width), 0)
    dst = lax.broadcasted_iota(jnp.int32, (LANES, width), 1)
    out = jnp.zeros((tc, width), F32)
    for j, piece in enumerate(_split_bf16(cum * LOG2E)):
        sel = jnp.where((dst // LANES == src) & (dst % LANES == j), 1.0, 0.0).astype(BF16)
        out = out + _dot(piece, sel)
    o_ref[...] = out.astype(o_ref.dtype)


def _cumgate(gates, bias, *, batch, heads):
    n, lanes = gates.shape
    t = n // batch
    tc = _tile(t, 512)
    nc = t // tc
    return pl.pallas_call(
        functools.partial(_cumgate_kernel, heads=heads),
        out_shape=jax.ShapeDtypeStruct((n, heads * LANES), BF16),
        grid=(batch, nc),
        in_specs=[pl.BlockSpec((tc, lanes), lambda b, c: (b * nc + c, 0)),
                  pl.BlockSpec((1, lanes), lambda b, c: (0, 0))],
        out_specs=pl.BlockSpec((tc, heads * LANES), lambda b, c: (b * nc + c, 0)),
        scratch_shapes=[pltpu.VMEM((8, lanes), F32)],
        compiler_params=_params("parallel", "arbitrary"),
        name="fox_cum_log_forget",
    )(gates, bias)


def _fox_kernel(q_ref, k_ref, ck_ref, v_ref, o_ref, s_ref, *, tk):
    qi = pl.program_id(2)
    tq = q_ref.shape[0]
    dh = FOX_HEAD_DIM
    nh = q_ref.shape[1] // dh
    assert tq == tk
    lane = lax.broadcasted_iota(jnp.int32, (tq, LANES), 1)
    minus_one = jnp.where(lane < BIAS_PIECES, -1.0, 0.0).astype(BF16)
    q_aug = [jnp.concatenate([q_ref[:, hd * dh:(hd + 1) * dh], minus_one], axis=1) for hd in range(nh)]

    def scores(ki, slot):
        start = pl.multiple_of(ki * tk, tk)
        for hd in range(nh):
            k_aug = jnp.concatenate([k_ref[pl.ds(start, tk), hd * dh:(hd + 1) * dh],
                                     ck_ref[pl.ds(start, tk), hd * LANES:(hd + 1) * LANES]], axis=1)
            s_ref[slot, hd] = _dot_nt(k_aug, q_aug[hd])

    def softmax_pv(ki, slot, carries, keep=None):
        start = pl.multiple_of(ki * tk, tk)
        out = []
        for hd in range(nh):
            m, l, acc = carries[hd]
            s = s_ref[slot, hd]
            if keep is not None:
                s = jnp.where(keep, s, -jnp.inf)
            m_new = jnp.maximum(m, jnp.max(s, axis=0, keepdims=True))
            alpha = jnp.exp2(m - m_new)
            p = jnp.exp2(s - m_new)
            l = alpha * l + jnp.sum(p, axis=0, keepdims=True)
            v = v_ref[pl.ds(start, tk), hd * dh:(hd + 1) * dh]
            acc = alpha * acc + _dot_tn(v, p.astype(BF16))
            out.append((m_new, l, acc))
        return tuple(out)

    init = tuple((jnp.full((1, tq), NEG_BIG, F32), jnp.zeros((1, tq), F32), jnp.zeros((dh, tq), F32))
                 for _ in range(nh))
    key = lax.broadcasted_iota(jnp.int32, (tk, tq), 0)
    qry = lax.broadcasted_iota(jnp.int32, (tk, tq), 1)
    scores(qi, 0)
    scores(0, 1)
    carries = softmax_pv(qi, 0, init, keep=key <= qry)

    def pair(j, carries):
        scores(2 * j + 1, 0)
        carries = softmax_pv(2 * j, 1, carries)
        scores(jnp.minimum(2 * j + 2, qi), 1)
        return softmax_pv(2 * j + 1, 0, carries)

    carries = lax.fori_loop(0, qi // 2, pair, carries)
    carries = lax.cond(qi % 2 == 1, lambda c: softmax_pv(qi - 1, 1, c), lambda c: c, carries)
    for hd, (m, l, acc) in enumerate(carries):
        o_ref[:, hd * dh:(hd + 1) * dh] = (acc * (1.0 / l)).T.astype(o_ref.dtype)


FOX_HEADS_PER_STEP = 2


def _fox_attention(proj, ck_aug, *, batch, d_model):
    n = proj.shape[0]
    t = n // batch
    dh = FOX_HEAD_DIM
    nh = FOX_HEADS_PER_STEP
    groups = d_model // (nh * dh)
    tq = _tile(t, 512)
    nq = t // tq
    return pl.pallas_call(
        functools.partial(_fox_kernel, tk=tq),
        out_shape=jax.ShapeDtypeStruct((n, d_model), BF16),
        grid=(batch, groups, nq),
        in_specs=[pl.BlockSpec((tq, nh * dh), lambda b, g, i: (b * nq + i, g)),
                  pl.BlockSpec((t, nh * dh), lambda b, g, i: (b, groups + g)),
                  pl.BlockSpec((t, nh * LANES), lambda b, g, i: (b, g)),
                  pl.BlockSpec((t, nh * dh), lambda b, g, i: (b, 2 * groups + g))],
        out_specs=pl.BlockSpec((tq, nh * dh), lambda b, g, i: (b * nq + i, g)),
        scratch_shapes=[pltpu.VMEM((2, nh, tq, tq), F32)],
        compiler_params=_params("parallel", "parallel", "arbitrary"),
        name="fox_attention",
    )(proj, proj, ck_aug, proj)


def _gate_weights_t(w_cols):
    g = w_cols.shape[1]
    return jnp.pad(w_cols.T, ((0, GATE_ROWS - g), (0, 0))).astype(BF16)


def _gate_bias(b):
    return jnp.pad(b.astype(F32), (0, GATE_ROWS - b.shape[0])).reshape(GATE_ROWS, 1)


def kernel(x, c, ada_w, ada_b, norm_mix_w, norm_ffn_w, mlstm_w_in, mlstm_b_gates, mlstm_norm_w, mlstm_w_out,
           fox_w_in, fox_b_f, fox_w_out, ffn_w_gate_up, ffn_w_down, final_norm_w):
    batch, t, d = x.shape
    depth = ada_w.shape[0]
    n = batch * t
    dqk = d // (2 * MLSTM_HEADS)
    mlstm_main = 2 * MLSTM_HEADS * dqk + 2 * d
    fox_main = 3 * d

    c_pad = jnp.pad(c, ((0, GATE_ROWS - batch), (0, 0)))
    mod = _modulation(c_pad, ada_w, ada_b)[:, :batch]
    mod = mod.reshape(depth, batch, 6, 1, d)

    xf = x.reshape(n, d)
    final_w = final_norm_w.reshape(1, d)
    for layer in range(depth):
        sh1, sc1, g1, sh2, sc2, g2 = [mod[layer, :, i] for i in range(6)]
        j = layer // 2
        nw = norm_mix_w[layer].reshape(1, d)
        if layer % 2 == 0:
            w_in = mlstm_w_in[j]
            proj, gates_t = _inproj(xf, nw, sc1, sh1, w_in[:, :mlstm_main].astype(BF16),
                                    _gate_weights_t(w_in[:, mlstm_main:]), batch=batch,
                                    q_cols=MLSTM_HEADS * dqk, q_scale=dqk ** -0.5, gates_head_major=True)
            y = _mlstm(proj, gates_t, _gate_bias(mlstm_b_gates[j]), mlstm_norm_w[j].reshape(1, d),
                       batch=batch, d_model=d)
            w_out = mlstm_w_out[j]
        else:
            w_in = fox_w_in[j]
            heads = d // FOX_HEAD_DIM
            wg = jnp.pad(w_in[:, fox_main:], ((0, 0), (0, LANES - heads))).astype(BF16)
            proj, gates = _inproj(xf, nw, sc1, sh1, w_in[:, :fox_main].astype(BF16), wg, batch=batch,
                                  q_cols=d, q_scale=FOX_HEAD_DIM ** -0.5 * LOG2E, gates_head_major=False)
            bias = jnp.pad(fox_b_f[j].astype(F32), (0, LANES - heads)).reshape(1, LANES)
            ck_aug = _cumgate(gates, bias, batch=batch, heads=heads)
            y = _fox_attention(proj, ck_aug, batch=batch, d_model=d)
            w_out = fox_w_out[j]
        xf = _outproj(y, w_out.astype(BF16), xf, g1, batch=batch)
        xf = _ffn(xf, norm_ffn_w[layer].reshape(1, d), sc2, sh2, g2,
                  ffn_w_gate_up[layer].astype(BF16), ffn_w_down[layer].astype(BF16), final_w,
                  batch=batch, final_norm=(layer == depth - 1))
    return xf.reshape(batch, t, d)
```

```python
import functools

import jax
import jax.numpy as jnp
from jax import lax
from jax.experimental import pallas as pl
from jax.experimental.pallas import tpu as pltpu

F32 = jnp.float32
BF16 = jnp.bfloat16

EPS = 1e-6
GATE_SOFTCAP = 15.0
MLSTM_HEADS = 4
FOX_HEAD_DIM = 128
GATE_ROWS = 16
LANES = 128
NEG_BIG = -1e30
LOG2E = 1.4426950408889634

VMEM_LIMIT = 56 * 1024 * 1024


def _params(*sem):
    return pltpu.CompilerParams(dimension_semantics=sem, vmem_limit_bytes=VMEM_LIMIT)


def _tile(n, pref):
    t = min(n, pref)
    assert n % t == 0, (n, pref)
    return t


def _log_sigmoid(x):
    return jnp.minimum(x, 0.0) - jnp.log1p(jnp.exp(-jnp.abs(x)))


def _sigmoid(x):
    return 1.0 / (1.0 + jnp.exp(-x))


def _dot(a, b):
    return jnp.dot(a, b, preferred_element_type=F32)


def _dot_nt(a, b):
    return lax.dot_general(a, b, (((1,), (1,)), ((), ())), preferred_element_type=F32)


def _dot_tn(a, b):
    return lax.dot_general(a, b, (((0,), (0,)), ((), ())), preferred_element_type=F32)


def _modulated_norm(x, nw, sc, sh):
    y = x * lax.rsqrt(jnp.mean(x * x, axis=-1, keepdims=True) + EPS)
    return (y * nw) * (1.0 + sc) + sh


def _mod_kernel(c_ref, w_ref, b_ref, o_ref):
    c = c_ref[...]
    cs = (c * _sigmoid(c)).astype(BF16)
    o_ref[0] = _dot(cs, w_ref[0].astype(BF16)) + b_ref[0]


def _modulation(c_pad, ada_w, ada_b):
    depth, d, n6 = ada_w.shape
    rows = c_pad.shape[0]
    tn = _tile(n6, 1024)
    return pl.pallas_call(
        _mod_kernel,
        out_shape=jax.ShapeDtypeStruct((depth, rows, n6), F32),
        grid=(depth, n6 // tn),
        in_specs=[pl.BlockSpec((rows, d), lambda l, j: (0, 0)),
                  pl.BlockSpec((1, d, tn), lambda l, j: (l, 0, j)),
                  pl.BlockSpec((1, 1, tn), lambda l, j: (l, 0, j))],
        out_specs=pl.BlockSpec((1, rows, tn), lambda l, j: (l, 0, j)),
        compiler_params=_params("parallel", "parallel"),
        name="adaln_modulation",
    )(c_pad, ada_w, ada_b.reshape(depth, 1, n6))


def _inproj_kernel(x_ref, nw_ref, sc_ref, sh_ref, w_ref, wg_ref, o_ref, g_ref, h_ref, *, q_tiles, q_scale,
                   gates_head_major):
    j = pl.program_id(1)

    def project(hb):
        scale = jnp.where(j < q_tiles, q_scale, 1.0).astype(F32)
        o_ref[...] = (_dot(hb, w_ref[...]) * scale).astype(o_ref.dtype)

    @pl.when(j == 0)
    def _():
        hb = _modulated_norm(x_ref[...], nw_ref[...], sc_ref[0], sh_ref[0]).astype(BF16)
        h_ref[...] = hb
        project(hb)
        if gates_head_major:
            g_ref[0] = _dot_nt(wg_ref[...], hb)
        else:
            g_ref[...] = _dot(hb, wg_ref[...])

    @pl.when(j > 0)
    def _():
        project(h_ref[...])


def _inproj(x, nw, sc, sh, w, wg, *, batch, q_cols, q_scale, gates_head_major):
    n, d = x.shape
    t = n // batch
    nout = w.shape[1]
    tm = _tile(t, 1024)
    tn = _tile(nout, 1024)
    assert q_cols % tn == 0
    tpb = t // tm
    kern = functools.partial(_inproj_kernel, q_tiles=q_cols // tn, q_scale=q_scale,
                             gates_head_major=gates_head_major)
    if gates_head_major:
        g_shape = jax.ShapeDtypeStruct((batch, GATE_ROWS, t), F32)
        g_spec = pl.BlockSpec((1, GATE_ROWS, tm), lambda i, j: (i // tpb, 0, i % tpb))
    else:
        g_shape = jax.ShapeDtypeStruct((n, LANES), F32)
        g_spec = pl.BlockSpec((tm, LANES), lambda i, j: (i, 0))
    return pl.pallas_call(
        kern,
        out_shape=(jax.ShapeDtypeStruct((n, nout), BF16), g_shape),
        grid=(n // tm, nout // tn),
        in_specs=[pl.BlockSpec((tm, d), lambda i, j: (i, 0)),
                  pl.BlockSpec((1, d), lambda i, j: (0, 0)),
                  pl.BlockSpec((1, 1, d), lambda i, j: (i // tpb, 0, 0)),
                  pl.BlockSpec((1, 1, d), lambda i, j: (i // tpb, 0, 0)),
                  pl.BlockSpec((d, tn), lambda i, j: (0, j)),
                  pl.BlockSpec(wg.shape, lambda i, j: (0, 0))],
        out_specs=(pl.BlockSpec((tm, tn), lambda i, j: (i, j)), g_spec),
        scratch_shapes=[pltpu.VMEM((tm, d), BF16)],
        compiler_params=_params("parallel", "arbitrary"),
        name="norm_inproj",
    )(x, nw, sc, sh, w, wg)


def _outproj_kernel(y_ref, w_ref, x_ref, g_ref, o_ref):
    o_ref[...] = x_ref[...] + g_ref[0] * _dot(y_ref[...], w_ref[...])


def _outproj(y, w, x, gate, *, batch):
    n, kdim = y.shape
    d = w.shape[1]
    t = n // batch
    tm = _tile(t, 512)
    tn = d
    tpb = t // tm
    return pl.pallas_call(
        _outproj_kernel,
        out_shape=jax.ShapeDtypeStruct((n, d), F32),
        grid=(n // tm, d // tn),
        in_specs=[pl.BlockSpec((tm, kdim), lambda i, j: (i, 0)),
                  pl.BlockSpec((kdim, tn), lambda i, j: (0, j)),
                  pl.BlockSpec((tm, tn), lambda i, j: (i, j)),
                  pl.BlockSpec((1, 1, tn), lambda i, j: (i // tpb, 0, j))],
        out_specs=pl.BlockSpec((tm, tn), lambda i, j: (i, j)),
        compiler_params=_params("parallel", "arbitrary"),
        name="outproj_residual",
    )(y, w, x, gate)


def _ffn_kernel(x_ref, nw_ref, sc_ref, sh_ref, g2_ref, wg_ref, wu_ref, wd_ref, fw_ref, o_ref, h_ref, *, final_norm):
    f = pl.program_id(1)

    def contribution(h):
        g = _dot(h, wg_ref[...])
        u = _dot(h, wu_ref[...])
        a = (g * _sigmoid(g) * u).astype(BF16)
        return g2_ref[0] * _dot(a, wd_ref[...])

    @pl.when(f == 0)
    def _():
        x = x_ref[...]
        hb = _modulated_norm(x, nw_ref[...], sc_ref[0], sh_ref[0]).astype(BF16)
        h_ref[...] = hb
        o_ref[...] = x + contribution(hb)

    @pl.when(f > 0)
    def _():
        o_ref[...] += contribution(h_ref[...])

    if final_norm:
        @pl.when(f == pl.num_programs(1) - 1)
        def _():
            out = o_ref[...]
            o_ref[...] = out * lax.rsqrt(jnp.mean(out * out, axis=-1, keepdims=True) + EPS) * fw_ref[...]


def _ffn(x, nw, sc, sh, g2, w_gate_up, w_down, final_w, *, batch, final_norm):
    n, d = x.shape
    dff = w_down.shape[0]
    t = n // batch
    tm = _tile(t, 1024)
    tf = _tile(dff, 256)
    tpb = t // tm
    nf = dff // tf
    vec = lambda: pl.BlockSpec((1, 1, d), lambda i, f: (i // tpb, 0, 0))
    return pl.pallas_call(
        functools.partial(_ffn_kernel, final_norm=final_norm),
        out_shape=jax.ShapeDtypeStruct((n, d), F32),
        grid=(n // tm, nf),
        in_specs=[pl.BlockSpec((tm, d), lambda i, f: (i, 0)),
                  pl.BlockSpec((1, d), lambda i, f: (0, 0)),
                  vec(), vec(), vec(),
                  pl.BlockSpec((d, tf), lambda i, f: (0, f)),
                  pl.BlockSpec((d, tf), lambda i, f: (0, nf + f)),
                  pl.BlockSpec((tf, d), lambda i, f: (f, 0)),
                  pl.BlockSpec((1, d), lambda i, f: (0, 0))],
        out_specs=pl.BlockSpec((tm, d), lambda i, f: (i, 0)),
        scratch_shapes=[pltpu.VMEM((tm, d), BF16)],
        compiler_params=_params("parallel", "arbitrary"),
        name="swiglu_ffn",
    )(x, nw, sc, sh, g2, w_gate_up, w_gate_up, w_down, final_w)


def _mlstm_kernel(q_ref, k_ref, v_ref, o_ref, g_ref, bias_ref, nw_ref, y_ref, c_ref, n_ref, m_ref, *, heads):
    ci = pl.program_id(1)
    L = q_ref.shape[0]
    dqk = q_ref.shape[1] // heads
    dv = v_ref.shape[1] // heads

    @pl.when(ci == 0)
    def _():
        c_ref[...] = jnp.zeros_like(c_ref)
        n_ref[...] = jnp.zeros_like(n_ref)
        m_ref[...] = jnp.zeros_like(m_ref)

    ones = jnp.ones((L, LANES), BF16)
    q = [q_ref[:, h * dqk:(h + 1) * dqk] for h in range(heads)]
    k = [k_ref[:, h * dqk:(h + 1) * dqk] for h in range(heads)]
    qk = [_dot_nt(q[h], k[h]) for h in range(heads)]
    qc = [_dot(q[h], c_ref[h].astype(BF16)) for h in range(heads)]
    qn = [_dot(q[h], n_ref[h].astype(BF16)) for h in range(heads)]

    gates = g_ref[0] + bias_ref[...]
    gates = GATE_SOFTCAP * jnp.tanh(gates / GATE_SOFTCAP)
    row = lax.broadcasted_iota(jnp.int32, (L, L), 0)
    col = lax.broadcasted_iota(jnp.int32, (L, L), 1)
    causal = row >= col
    eye = row == col

    for h in range(heads):
        li_row = gates[h:h + 1, :]
        lf_row = _log_sigmoid(gates[heads + h:heads + h + 1, :])
        b_col = jnp.sum(jnp.where(causal, lf_row, 0.0), axis=1, keepdims=True)
        b_row = jnp.sum(jnp.where(eye, b_col, 0.0), axis=0, keepdims=True)
        li_col = jnp.sum(jnp.where(eye, li_row, 0.0), axis=1, keepdims=True)
        g_tot = jnp.sum(lf_row, axis=1, keepdims=True)

        m_prev = m_ref[h:h + 1, 0:1]
        dmat = jnp.where(causal, b_col + (li_row - b_row), -jnp.inf)
        inter_log = b_col + m_prev
        m_t = jnp.maximum(inter_log, jnp.max(dmat, axis=1, keepdims=True))
        w_intra = jnp.exp(dmat - m_t)
        w_inter = jnp.exp(inter_log - m_t)

        v = v_ref[:, h * dv:(h + 1) * dv]
        s = (qk[h] * w_intra).astype(BF16)
        num = w_inter * qc[h] + _dot(s, v)
        den = w_inter * qn[h][:, 0:1] + _dot(s, ones)[:, 0:1]
        hid = num * (1.0 / jnp.maximum(jnp.abs(den), jnp.exp(-m_t)))

        tail = g_tot - b_col + li_col
        m_new = jnp.maximum(g_tot + m_prev, jnp.max(tail, axis=0, keepdims=True))
        w_tail = jnp.exp(tail - m_new)
        decay = jnp.exp(g_tot + m_prev - m_new)
        kw = (k[h].astype(F32) * w_tail).astype(BF16)
        c_ref[h] = decay * c_ref[h] + _dot_tn(kw, v)
        n_ref[h] = decay * n_ref[h] + _dot_tn(kw, ones)
        m_ref[h:h + 1, :] = jnp.broadcast_to(m_new, (1, LANES))

        hs = hid * lax.rsqrt(jnp.mean(hid * hid, axis=-1, keepdims=True) + EPS) * nw_ref[:, h * dv:(h + 1) * dv]
        gate = _sigmoid(o_ref[:, h * dv:(h + 1) * dv].astype(F32))
        y_ref[:, h * dv:(h + 1) * dv] = (gate * hs).astype(y_ref.dtype)


def _mlstm(proj, gates_t, bias, norm_w, *, batch, d_model):
    n = proj.shape[0]
    t = n // batch
    heads = MLSTM_HEADS
    dqk = d_model // (2 * heads)
    dv = d_model // heads
    L = _tile(t, 256)
    nc = t // L
    qk_w = heads * dqk
    assert (2 * qk_w) % (heads * dv) == 0 and heads <= 8
    v_blk = (2 * qk_w) // (heads * dv)
    return pl.pallas_call(
        functools.partial(_mlstm_kernel, heads=heads),
        out_shape=jax.ShapeDtypeStruct((n, heads * dv), BF16),
        grid=(batch, nc),
        in_specs=[pl.BlockSpec((L, qk_w), lambda b, c: (b * nc + c, 0)),
                  pl.BlockSpec((L, qk_w), lambda b, c: (b * nc + c, 1)),
                  pl.BlockSpec((L, heads * dv), lambda b, c: (b * nc + c, v_blk)),
                  pl.BlockSpec((L, heads * dv), lambda b, c: (b * nc + c, v_blk + 1)),
                  pl.BlockSpec((1, GATE_ROWS, L), lambda b, c: (b, 0, c)),
                  pl.BlockSpec((GATE_ROWS, 1), lambda b, c: (0, 0)),
                  pl.BlockSpec((1, heads * dv), lambda b, c: (0, 0))],
        out_specs=pl.BlockSpec((L, heads * dv), lambda b, c: (b * nc + c, 0)),
        scratch_shapes=[pltpu.VMEM((heads, dqk, dv), F32), pltpu.VMEM((heads, dqk, LANES), F32),
                        pltpu.VMEM((8, LANES), F32)],
        compiler_params=_params("parallel", "arbitrary"),
        name="mlstm_chunkwise",
    )(proj, proj, proj, proj, gates_t, bias, norm_w)


BIAS_PIECES = 3


def _split_bf16(x):
    p0 = x.astype(BF16)
    r1 = x - p0.astype(F32)
    p1 = r1.astype(BF16)
    p2 = (r1 - p1.astype(F32)).astype(BF16)
    return p0, p1, p2


def _cumgate_kernel(g_ref, bias_ref, o_ref, carry_ref, *, group):
    @pl.when(pl.program_id(1) == 0)
    def _():
        carry_ref[...] = jnp.zeros_like(carry_ref)

    lf = _log_sigmoid(g_ref[...] + bias_ref[...])
    tc = lf.shape[0]
    row = lax.broadcasted_iota(jnp.int32, (tc, tc), 0)
    col = lax.broadcasted_iota(jnp.int32, (tc, tc), 1)
    lower = jnp.where(row >= col, 1.0, 0.0).astype(BF16)
    cum = carry_ref[0:1, :]
    for piece in _split_bf16(lf):
        cum = cum + _dot(lower, piece)
    carry_ref[...] = jnp.broadcast_to(cum[tc - 1:tc, :], carry_ref.shape)

    width = o_ref.shape[1]
    src = lax.broadcasted_iota(jnp.int32, (LANES, width), 0)
    dst = lax.broadcasted_iota(jnp.int32, (LANES, width), 1)
    dst_head = (dst // LANES) * group + dst % group
    dst_piece = (dst % LANES) // group
    out = jnp.zeros((tc, width), F32)
    for j, piece in enumerate(_split_bf16(cum * LOG2E)):
        sel = jnp.where((dst_head == src) & (dst_piece == j), 1.0, 0.0).astype(BF16)
        out = out + _dot(piece, sel)
    o_ref[...] = out.astype(o_ref.dtype)


def _cumgate(gates, bias, *, batch, heads, group):
    n, lanes = gates.shape
    t = n // batch
    tc = _tile(t, 512)
    nc = t // tc
    assert heads % group == 0 and group & (group - 1) == 0 and BIAS_PIECES * group <= LANES
    width = (heads // group) * LANES
    return pl.pallas_call(
        functools.partial(_cumgate_kernel, group=group),
        out_shape=jax.ShapeDtypeStruct((n, width), BF16),
        grid=(batch, nc),
        in_specs=[pl.BlockSpec((tc, lanes), lambda b, c: (b * nc + c, 0)),
                  pl.BlockSpec((1, lanes), lambda b, c: (0, 0))],
        out_specs=pl.BlockSpec((tc, width), lambda b, c: (b * nc + c, 0)),
        scratch_shapes=[pltpu.VMEM((8, lanes), F32)],
        compiler_params=_params("parallel", "arbitrary"),
        name="fox_cum_log_forget",
    )(gates, bias)


def _fox_kernel(q_ref, k_ref, ck_ref, v_ref, o_ref, s_ref, *, tk):
    qi = pl.program_id(2)
    tq = q_ref.shape[0]
    dh = FOX_HEAD_DIM
    nh = q_ref.shape[1] // dh
    assert tq == tk
    lane = lax.broadcasted_iota(jnp.int32, (tq, LANES), 1)
    q_aug = []
    for hd in range(nh):
        minus_one = jnp.where((lane % nh == hd) & (lane < BIAS_PIECES * nh), -1.0, 0.0).astype(BF16)
        q_aug.append(jnp.concatenate([q_ref[:, hd * dh:(hd + 1) * dh], minus_one], axis=1))

    def scores(ki, slot):
        start = pl.multiple_of(ki * tk, tk)
        ck = ck_ref[pl.ds(start, tk), :]
        for hd in range(nh):
            k_aug = jnp.concatenate([k_ref[pl.ds(start, tk), hd * dh:(hd + 1) * dh], ck], axis=1)
            s_ref[slot, hd] = _dot_nt(k_aug, q_aug[hd])

    def softmax_pv(ki, slot, carries, keep=None):
        start = pl.multiple_of(ki * tk, tk)
        out = []
        for hd in range(nh):
            m, l, acc = carries[hd]
            s = s_ref[slot, hd]
            if keep is not None:
                s = jnp.where(keep, s, -jnp.inf)
            m_new = jnp.maximum(m, jnp.max(s, axis=0, keepdims=True))
            alpha = jnp.exp2(m - m_new)
            p = jnp.exp2(s - m_new)
            l = alpha * l + jnp.sum(p, axis=0, keepdims=True)
            v = v_ref[pl.ds(start, tk), hd * dh:(hd + 1) * dh]
            acc = alpha * acc + _dot_tn(v, p.astype(BF16))
            out.append((m_new, l, acc))
        return tuple(out)

    init = tuple((jnp.full((1, tq), NEG_BIG, F32), jnp.zeros((1, tq), F32), jnp.zeros((dh, tq), F32))
                 for _ in range(nh))
    key = lax.broadcasted_iota(jnp.int32, (tk, tq), 0)
    qry = lax.broadcasted_iota(jnp.int32, (tk, tq), 1)
    scores(qi, 0)
    scores(0, 1)
    carries = softmax_pv(qi, 0, init, keep=key <= qry)

    def pair(j, carries):
        scores(2 * j + 1, 0)
        carries = softmax_pv(2 * j, 1, carries)
        scores(jnp.minimum(2 * j + 2, qi), 1)
        return softmax_pv(2 * j + 1, 0, carries)

    carries = lax.fori_loop(0, qi // 2, pair, carries)
    carries = lax.cond(qi % 2 == 1, lambda c: softmax_pv(qi - 1, 1, c), lambda c: c, carries)
    for hd, (m, l, acc) in enumerate(carries):
        o_ref[:, hd * dh:(hd + 1) * dh] = (acc * (1.0 / l)).T.astype(o_ref.dtype)


FOX_HEADS_PER_STEP = 4


def _fox_attention(proj, ck_aug, *, batch, d_model):
    n = proj.shape[0]
    t = n // batch
    dh = FOX_HEAD_DIM
    nh = FOX_HEADS_PER_STEP
    groups = d_model // (nh * dh)
    assert ck_aug.shape[1] == groups * LANES
    tq = _tile(t, 512)
    nq = t // tq
    return pl.pallas_call(
        functools.partial(_fox_kernel, tk=tq),
        out_shape=jax.ShapeDtypeStruct((n, d_model), BF16),
        grid=(batch, groups, nq),
        in_specs=[pl.BlockSpec((tq, nh * dh), lambda b, g, i: (b * nq + i, g)),
                  pl.BlockSpec((t, nh * dh), lambda b, g, i: (b, groups + g)),
                  pl.BlockSpec((t, LANES), lambda b, g, i: (b, g)),
                  pl.BlockSpec((t, nh * dh), lambda b, g, i: (b, 2 * groups + g))],
        out_specs=pl.BlockSpec((tq, nh * dh), lambda b, g, i: (b * nq + i, g)),
        scratch_shapes=[pltpu.VMEM((2, nh, tq, tq), F32)],
        compiler_params=_params("parallel", "parallel", "arbitrary"),
        name="fox_attention",
    )(proj, proj, ck_aug, proj)


def _gate_weights_t(w_cols):
    g = w_cols.shape[1]
    return jnp.pad(w_cols.T, ((0, GATE_ROWS - g), (0, 0))).astype(BF16)


def _gate_bias(b):
    return jnp.pad(b.astype(F32), (0, GATE_ROWS - b.shape[0])).reshape(GATE_ROWS, 1)


def kernel(x, c, ada_w, ada_b, norm_mix_w, norm_ffn_w, mlstm_w_in, mlstm_b_gates, mlstm_norm_w, mlstm_w_out,
           fox_w_in, fox_b_f, fox_w_out, ffn_w_gate_up, ffn_w_down, final_norm_w):
    batch, t, d = x.shape
    depth = ada_w.shape[0]
    n = batch * t
    dqk = d // (2 * MLSTM_HEADS)
    mlstm_main = 2 * MLSTM_HEADS * dqk + 2 * d
    fox_main = 3 * d

    c_pad = jnp.pad(c, ((0, GATE_ROWS - batch), (0, 0)))
    mod = _modulation(c_pad, ada_w, ada_b)[:, :batch]
    mod = mod.reshape(depth, batch, 6, 1, d)

    xf = x.reshape(n, d)
    final_w = final_norm_w.reshape(1, d)
    for layer in range(depth):
        sh1, sc1, g1, sh2, sc2, g2 = [mod[layer, :, i] for i in range(6)]
        j = layer // 2
        nw = norm_mix_w[layer].reshape(1, d)
        if layer % 2 == 0:
            w_in = mlstm_w_in[j]
            proj, gates_t = _inproj(xf, nw, sc1, sh1, w_in[:, :mlstm_main].astype(BF16),
                                    _gate_weights_t(w_in[:, mlstm_main:]), batch=batch,
                                    q_cols=MLSTM_HEADS * dqk, q_scale=dqk ** -0.5, gates_head_major=True)
            y = _mlstm(proj, gates_t, _gate_bias(mlstm_b_gates[j]), mlstm_norm_w[j].reshape(1, d),
                       batch=batch, d_model=d)
            w_out = mlstm_w_out[j]
        else:
            w_in = fox_w_in[j]
            heads = d // FOX_HEAD_DIM
            wg = jnp.pad(w_in[:, fox_main:], ((0, 0), (0, LANES - heads))).astype(BF16)
            proj, gates = _inproj(xf, nw, sc1, sh1, w_in[:, :fox_main].astype(BF16), wg, batch=batch,
                                  q_cols=d, q_scale=FOX_HEAD_DIM ** -0.5 * LOG2E, gates_head_major=False)
            bias = jnp.pad(fox_b_f[j].astype(F32), (0, LANES - heads)).reshape(1, LANES)
            ck_aug = _cumgate(gates, bias, batch=batch, heads=heads, group=FOX_HEADS_PER_STEP)
            y = _fox_attention(proj, ck_aug, batch=batch, d_model=d)
            w_out = fox_w_out[j]
        xf = _outproj(y, w_out.astype(BF16), xf, g1, batch=batch)
        xf = _ffn(xf, norm_ffn_w[layer].reshape(1, d), sc2, sh2, g2,
                  ffn_w_gate_up[layer].astype(BF16), ffn_w_down[layer].astype(BF16), final_w,
                  batch=batch, final_norm=(layer == depth - 1))
    return xf.reshape(batch, t, d)
```

```python
import functools

import jax
import jax.numpy as jnp
from jax import lax
from jax.experimental import pallas as pl
from jax.experimental.pallas import tpu as pltpu

F32 = jnp.float32
BF16 = jnp.bfloat16

EPS = 1e-6
GATE_SOFTCAP = 15.0
MLSTM_HEADS = 4
FOX_HEAD_DIM = 128
GATE_ROWS = 16
LANES = 128
NEG_BIG = -1e30
LOG2E = 1.4426950408889634

VMEM_LIMIT = 56 * 1024 * 1024


def _params(*sem):
    return pltpu.CompilerParams(dimension_semantics=sem, vmem_limit_bytes=VMEM_LIMIT)


def _tile(n, pref):
    t = min(n, pref)
    assert n % t == 0, (n, pref)
    return t


def _log_sigmoid(x):
    return jnp.minimum(x, 0.0) - jnp.log1p(jnp.exp(-jnp.abs(x)))


def _sigmoid(x):
    return 1.0 / (1.0 + jnp.exp(-x))


def _dot(a, b):
    return jnp.dot(a, b, preferred_element_type=F32)


def _dot_nt(a, b):
    return lax.dot_general(a, b, (((1,), (1,)), ((), ())), preferred_element_type=F32)


def _dot_tn(a, b):
    return lax.dot_general(a, b, (((0,), (0,)), ((), ())), preferred_element_type=F32)


def _modulated_norm(x, nw, sc, sh):
    y = x * lax.rsqrt(jnp.mean(x * x, axis=-1, keepdims=True) + EPS)
    return (y * nw) * (1.0 + sc) + sh


def _mod_kernel(c_ref, w_ref, b_ref, o_ref):
    c = c_ref[...]
    cs = (c * _sigmoid(c)).astype(BF16)
    o_ref[0] = _dot(cs, w_ref[0].astype(BF16)) + b_ref[0]


def _modulation(c_pad, ada_w, ada_b):
    depth, d, n6 = ada_w.shape
    rows = c_pad.shape[0]
    tn = _tile(n6, 1024)
    return pl.pallas_call(
        _mod_kernel,
        out_shape=jax.ShapeDtypeStruct((depth, rows, n6), F32),
        grid=(depth, n6 // tn),
        in_specs=[pl.BlockSpec((rows, d), lambda l, j: (0, 0)),
                  pl.BlockSpec((1, d, tn), lambda l, j: (l, 0, j)),
                  pl.BlockSpec((1, 1, tn), lambda l, j: (l, 0, j))],
        out_specs=pl.BlockSpec((1, rows, tn), lambda l, j: (l, 0, j)),
        compiler_params=_params("parallel", "parallel"),
        name="adaln_modulation",
    )(c_pad, ada_w, ada_b.reshape(depth, 1, n6))


def _inproj_kernel(x_ref, nw_ref, sc_ref, sh_ref, w_ref, wg_ref, o_ref, g_ref, h_ref, *, q_tiles, q_scale,
                   gates_head_major):
    j = pl.program_id(1)

    def project(hb):
        scale = jnp.where(j < q_tiles, q_scale, 1.0).astype(F32)
        o_ref[...] = (_dot(hb, w_ref[0]) * scale).astype(o_ref.dtype)

    @pl.when(j == 0)
    def _():
        hb = _modulated_norm(x_ref[...], nw_ref[...], sc_ref[0], sh_ref[0]).astype(BF16)
        h_ref[...] = hb
        project(hb)
        if gates_head_major:
            g_ref[0] = _dot_nt(wg_ref[...], hb)
        else:
            g_ref[...] = _dot(hb, wg_ref[...])

    @pl.when(j > 0)
    def _():
        project(h_ref[...])


def _inproj(x, nw, sc, sh, w, layer, nout, wg, *, batch, q_cols, q_scale, gates_head_major):
    n, d = x.shape
    t = n // batch
    tm = _tile(t, 1024)
    tn = _tile(nout, 1024)
    assert q_cols % tn == 0
    tpb = t // tm
    kern = functools.partial(_inproj_kernel, q_tiles=q_cols // tn, q_scale=q_scale,
                             gates_head_major=gates_head_major)
    if gates_head_major:
        g_shape = jax.ShapeDtypeStruct((batch, GATE_ROWS, t), F32)
        g_spec = pl.BlockSpec((1, GATE_ROWS, tm), lambda i, j: (i // tpb, 0, i % tpb))
    else:
        g_shape = jax.ShapeDtypeStruct((n, LANES), F32)
        g_spec = pl.BlockSpec((tm, LANES), lambda i, j: (i, 0))
    return pl.pallas_call(
        kern,
        out_shape=(jax.ShapeDtypeStruct((n, nout), BF16), g_shape),
        grid=(n // tm, nout // tn),
        in_specs=[pl.BlockSpec((tm, d), lambda i, j: (i, 0)),
                  pl.BlockSpec((1, d), lambda i, j: (0, 0)),
                  pl.BlockSpec((1, 1, d), lambda i, j: (i // tpb, 0, 0)),
                  pl.BlockSpec((1, 1, d), lambda i, j: (i // tpb, 0, 0)),
                  pl.BlockSpec((1, d, tn), lambda i, j: (layer, 0, j)),
                  pl.BlockSpec(wg.shape, lambda i, j: (0, 0))],
        out_specs=(pl.BlockSpec((tm, tn), lambda i, j: (i, j)), g_spec),
        scratch_shapes=[pltpu.VMEM((tm, d), BF16)],
        compiler_params=_params("parallel", "arbitrary"),
        name="norm_inproj",
    )(x, nw, sc, sh, w, wg)


def _outproj_kernel(y_ref, w_ref, x_ref, g_ref, o_ref):
    o_ref[...] = x_ref[...] + g_ref[0] * _dot(y_ref[...], w_ref[0])


def _outproj(y, w, layer, x, gate, *, batch):
    n, kdim = y.shape
    d = w.shape[2]
    t = n // batch
    tm = _tile(t, 512)
    tn = d
    tpb = t // tm
    return pl.pallas_call(
        _outproj_kernel,
        out_shape=jax.ShapeDtypeStruct((n, d), F32),
        grid=(n // tm, d // tn),
        in_specs=[pl.BlockSpec((tm, kdim), lambda i, j: (i, 0)),
                  pl.BlockSpec((1, kdim, tn), lambda i, j: (layer, 0, j)),
                  pl.BlockSpec((tm, tn), lambda i, j: (i, j)),
                  pl.BlockSpec((1, 1, tn), lambda i, j: (i // tpb, 0, j))],
        out_specs=pl.BlockSpec((tm, tn), lambda i, j: (i, j)),
        compiler_params=_params("parallel", "arbitrary"),
        name="outproj_residual",
    )(y, w, x, gate)


def _ffn_kernel(x_ref, nw_ref, sc_ref, sh_ref, g2_ref, wg_ref, wu_ref, wd_ref, fw_ref, o_ref, h_ref, *, final_norm):
    f = pl.program_id(1)

    def contribution(h):
        g = _dot(h, wg_ref[0])
        u = _dot(h, wu_ref[0])
        a = (g * _sigmoid(g) * u).astype(BF16)
        return g2_ref[0] * _dot(a, wd_ref[0])

    @pl.when(f == 0)
    def _():
        x = x_ref[...]
        hb = _modulated_norm(x, nw_ref[...], sc_ref[0], sh_ref[0]).astype(BF16)
        h_ref[...] = hb
        o_ref[...] = x + contribution(hb)

    @pl.when(f > 0)
    def _():
        o_ref[...] += contribution(h_ref[...])

    if final_norm:
        @pl.when(f == pl.num_programs(1) - 1)
        def _():
            out = o_ref[...]
            o_ref[...] = out * lax.rsqrt(jnp.mean(out * out, axis=-1, keepdims=True) + EPS) * fw_ref[...]


def _ffn(x, nw, sc, sh, g2, w_gate_up, w_down, layer, final_w, *, batch, final_norm):
    n, d = x.shape
    dff = w_down.shape[1]
    t = n // batch
    tm = _tile(t, 1024)
    tf = _tile(dff, 256)
    tpb = t // tm
    nf = dff // tf
    vec = lambda: pl.BlockSpec((1, 1, d), lambda i, f: (i // tpb, 0, 0))
    return pl.pallas_call(
        functools.partial(_ffn_kernel, final_norm=final_norm),
        out_shape=jax.ShapeDtypeStruct((n, d), F32),
        grid=(n // tm, nf),
        in_specs=[pl.BlockSpec((tm, d), lambda i, f: (i, 0)),
                  pl.BlockSpec((1, d), lambda i, f: (0, 0)),
                  vec(), vec(), vec(),
                  pl.BlockSpec((1, d, tf), lambda i, f: (layer, 0, f)),
                  pl.BlockSpec((1, d, tf), lambda i, f: (layer, 0, nf + f)),
                  pl.BlockSpec((1, tf, d), lambda i, f: (layer, f, 0)),
                  pl.BlockSpec((1, d), lambda i, f: (0, 0))],
        out_specs=pl.BlockSpec((tm, d), lambda i, f: (i, 0)),
        scratch_shapes=[pltpu.VMEM((tm, d), BF16)],
        compiler_params=_params("parallel", "arbitrary"),
        name="swiglu_ffn",
    )(x, nw, sc, sh, g2, w_gate_up, w_gate_up, w_down, final_w)


def _mlstm_kernel(q_ref, k_ref, v_ref, o_ref, g_ref, bias_ref, nw_ref, y_ref, c_ref, n_ref, m_ref, *, heads):
    ci = pl.program_id(1)
    L = q_ref.shape[0]
    dqk = q_ref.shape[1] // heads
    dv = v_ref.shape[1] // heads

    @pl.when(ci == 0)
    def _():
        c_ref[...] = jnp.zeros_like(c_ref)
        n_ref[...] = jnp.zeros_like(n_ref)
        m_ref[...] = jnp.zeros_like(m_ref)

    ones = jnp.ones((L, LANES), BF16)
    q = [q_ref[:, h * dqk:(h + 1) * dqk] for h in range(heads)]
    k = [k_ref[:, h * dqk:(h + 1) * dqk] for h in range(heads)]
    qk = [_dot_nt(q[h], k[h]) for h in range(heads)]
    qc = [_dot(q[h], c_ref[h].astype(BF16)) for h in range(heads)]
    qn = [_dot(q[h], n_ref[h].astype(BF16)) for h in range(heads)]

    gates = g_ref[0] + bias_ref[...]
    gates = GATE_SOFTCAP * jnp.tanh(gates / GATE_SOFTCAP)
    row = lax.broadcasted_iota(jnp.int32, (L, L), 0)
    col = lax.broadcasted_iota(jnp.int32, (L, L), 1)
    causal = row >= col
    eye = row == col

    for h in range(heads):
        li_row = gates[h:h + 1, :]
        lf_row = _log_sigmoid(gates[heads + h:heads + h + 1, :])
        b_col = jnp.sum(jnp.where(causal, lf_row, 0.0), axis=1, keepdims=True)
        b_row = jnp.sum(jnp.where(eye, b_col, 0.0), axis=0, keepdims=True)
        li_col = jnp.sum(jnp.where(eye, li_row, 0.0), axis=1, keepdims=True)
        g_tot = jnp.sum(lf_row, axis=1, keepdims=True)

        m_prev = m_ref[h:h + 1, 0:1]
        dmat = jnp.where(causal, b_col + (li_row - b_row), -jnp.inf)
        inter_log = b_col + m_prev
        m_t = jnp.maximum(inter_log, jnp.max(dmat, axis=1, keepdims=True))
        w_intra = jnp.exp(dmat - m_t)
        w_inter = jnp.exp(inter_log - m_t)

        v = v_ref[:, h * dv:(h + 1) * dv]
        s = (qk[h] * w_intra).astype(BF16)
        num = w_inter * qc[h] + _dot(s, v)
        den = w_inter * qn[h][:, 0:1] + _dot(s, ones)[:, 0:1]
        hid = num * (1.0 / jnp.maximum(jnp.abs(den), jnp.exp(-m_t)))

        tail = g_tot - b_col + li_col
        m_new = jnp.maximum(g_tot + m_prev, jnp.max(tail, axis=0, keepdims=True))
        w_tail = jnp.exp(tail - m_new)
        decay = jnp.exp(g_tot + m_prev - m_new)
        kw = (k[h].astype(F32) * w_tail).astype(BF16)
        c_ref[h] = decay * c_ref[h] + _dot_tn(kw, v)
        n_ref[h] = decay * n_ref[h] + _dot_tn(kw, ones)
        m_ref[h:h + 1, :] = jnp.broadcast_to(m_new, (1, LANES))

        hs = hid * lax.rsqrt(jnp.mean(hid * hid, axis=-1, keepdims=True) + EPS) * nw_ref[:, h * dv:(h + 1) * dv]
        gate = _sigmoid(o_ref[:, h * dv:(h + 1) * dv].astype(F32))
        y_ref[:, h * dv:(h + 1) * dv] = (gate * hs).astype(y_ref.dtype)


def _mlstm(proj, gates_t, bias, norm_w, *, batch, d_model):
    n = proj.shape[0]
    t = n // batch
    heads = MLSTM_HEADS
    dqk = d_model // (2 * heads)
    dv = d_model // heads
    L = _tile(t, 256)
    nc = t // L
    qk_w = heads * dqk
    assert (2 * qk_w) % (heads * dv) == 0 and heads <= 8
    v_blk = (2 * qk_w) // (heads * dv)
    return pl.pallas_call(
        functools.partial(_mlstm_kernel, heads=heads),
        out_shape=jax.ShapeDtypeStruct((n, heads * dv), BF16),
        grid=(batch, nc),
        in_specs=[pl.BlockSpec((L, qk_w), lambda b, c: (b * nc + c, 0)),
                  pl.BlockSpec((L, qk_w), lambda b, c: (b * nc + c, 1)),
                  pl.BlockSpec((L, heads * dv), lambda b, c: (b * nc + c, v_blk)),
                  pl.BlockSpec((L, heads * dv), lambda b, c: (b * nc + c, v_blk + 1)),
                  pl.BlockSpec((1, GATE_ROWS, L), lambda b, c: (b, 0, c)),
                  pl.BlockSpec((GATE_ROWS, 1), lambda b, c: (0, 0)),
                  pl.BlockSpec((1, heads * dv), lambda b, c: (0, 0))],
        out_specs=pl.BlockSpec((L, heads * dv), lambda b, c: (b * nc + c, 0)),
        scratch_shapes=[pltpu.VMEM((heads, dqk, dv), F32), pltpu.VMEM((heads, dqk, LANES), F32),
                        pltpu.VMEM((8, LANES), F32)],
        compiler_params=_params("parallel", "arbitrary"),
        name="mlstm_chunkwise",
    )(proj, proj, proj, proj, gates_t, bias, norm_w)


BIAS_PIECES = 3


def _split_bf16(x):
    p0 = x.astype(BF16)
    r1 = x - p0.astype(F32)
    p1 = r1.astype(BF16)
    p2 = (r1 - p1.astype(F32)).astype(BF16)
    return p0, p1, p2


def _cumgate_kernel(g_ref, bias_ref, o_ref, carry_ref, *, group):
    @pl.when(pl.program_id(1) == 0)
    def _():
        carry_ref[...] = jnp.zeros_like(carry_ref)

    lf = _log_sigmoid(g_ref[...] + bias_ref[...])
    tc = lf.shape[0]
    row = lax.broadcasted_iota(jnp.int32, (tc, tc), 0)
    col = lax.broadcasted_iota(jnp.int32, (tc, tc), 1)
    lower = jnp.where(row >= col, 1.0, 0.0).astype(BF16)
    cum = carry_ref[0:1, :]
    for piece in _split_bf16(lf):
        cum = cum + _dot(lower, piece)
    carry_ref[...] = jnp.broadcast_to(cum[tc - 1:tc, :], carry_ref.shape)

    width = o_ref.shape[1]
    src = lax.broadcasted_iota(jnp.int32, (LANES, width), 0)
    dst = lax.broadcasted_iota(jnp.int32, (LANES, width), 1)
    dst_head = (dst // LANES) * group + dst % group
    dst_piece = (dst % LANES) // group
    out = jnp.zeros((tc, width), F32)
    for j, piece in enumerate(_split_bf16(cum * LOG2E)):
        sel = jnp.where((dst_head == src) & (dst_piece == j), 1.0, 0.0).astype(BF16)
        out = out + _dot(piece, sel)
    o_ref[...] = out.astype(o_ref.dtype)


def _cumgate(gates, bias, *, batch, heads, group):
    n, lanes = gates.shape
    t = n // batch
    tc = _tile(t, 512)
    nc = t // tc
    assert heads % group == 0 and group & (group - 1) == 0 and BIAS_PIECES * group <= LANES
    width = (heads // group) * LANES
    return pl.pallas_call(
        functools.partial(_cumgate_kernel, group=group),
        out_shape=jax.ShapeDtypeStruct((n, width), BF16),
        grid=(batch, nc),
        in_specs=[pl.BlockSpec((tc, lanes), lambda b, c: (b * nc + c, 0)),
                  pl.BlockSpec((1, lanes), lambda b, c: (0, 0))],
        out_specs=pl.BlockSpec((tc, width), lambda b, c: (b * nc + c, 0)),
        scratch_shapes=[pltpu.VMEM((8, lanes), F32)],
        compiler_params=_params("parallel", "arbitrary"),
        name="fox_cum_log_forget",
    )(gates, bias)


def _fox_kernel(q_ref, k_ref, ck_ref, v_ref, o_ref, s_ref, m_ref, l_ref, acc_ref, *, tk):
    qi = pl.program_id(2)
    tq = q_ref.shape[0]
    dh = FOX_HEAD_DIM
    nh = q_ref.shape[1] // dh
    assert tq == tk
    lane = lax.broadcasted_iota(jnp.int32, (tq, LANES), 1)
    q_aug = []
    for hd in range(nh):
        minus_one = jnp.where((lane % nh == hd) & (lane < BIAS_PIECES * nh), -1.0, 0.0).astype(BF16)
        q_aug.append(jnp.concatenate([q_ref[:, hd * dh:(hd + 1) * dh], minus_one], axis=1))

    def head_scores(hd, ki, slot):
        start = pl.multiple_of(ki * tk, tk)
        k_aug = jnp.concatenate([k_ref[pl.ds(start, tk), hd * dh:(hd + 1) * dh],
                                 ck_ref[pl.ds(start, tk), :]], axis=1)
        s_ref[slot, hd] = _dot_nt(k_aug, q_aug[hd])

    def head_softmax_pv(hd, ki, slot, keep, first):
        start = pl.multiple_of(ki * tk, tk)
        s = s_ref[slot, hd]
        if keep is not None:
            s = jnp.where(keep, s, -jnp.inf)
        m_blk = jnp.max(s, axis=0, keepdims=True)
        m_new = m_blk if first else jnp.maximum(m_ref[hd], m_blk)
        p = jnp.exp2(s - m_new)
        l_blk = jnp.sum(p, axis=0, keepdims=True)
        pv = _dot_tn(v_ref[pl.ds(start, tk), hd * dh:(hd + 1) * dh], p.astype(BF16))
        if first:
            l_ref[hd] = l_blk
            acc_ref[hd] = pv
        else:
            alpha = jnp.exp2(m_ref[hd] - m_new)
            l_ref[hd] = alpha * l_ref[hd] + l_blk
            acc_ref[hd] = alpha * acc_ref[hd] + pv
        m_ref[hd] = m_new

    def scores(ki, slot):
        for hd in range(nh):
            head_scores(hd, ki, slot)

    def softmax_pv(ki, slot, keep=None, next_block=None, first=False):
        for hd in range(nh):
            if next_block is not None:
                head_scores(hd, *next_block)
            head_softmax_pv(hd, ki, slot, keep, first)

    key = lax.broadcasted_iota(jnp.int32, (tk, tq), 0)
    qry = lax.broadcasted_iota(jnp.int32, (tk, tq), 1)
    scores(qi, 0)
    softmax_pv(qi, 0, keep=key <= qry, next_block=(0, 1), first=True)

    def pair(j, carry):
        softmax_pv(2 * j, 1, next_block=(2 * j + 1, 0))
        softmax_pv(2 * j + 1, 0, next_block=(jnp.minimum(2 * j + 2, qi), 1))
        return carry

    lax.fori_loop(0, qi // 2, pair, 0)

    @pl.when(qi % 2 == 1)
    def _():
        softmax_pv(qi - 1, 1)

    for hd in range(nh):
        o_ref[:, hd * dh:(hd + 1) * dh] = (acc_ref[hd] * (1.0 / l_ref[hd])).T.astype(o_ref.dtype)


FOX_HEADS_PER_STEP = 4


def _fox_attention(proj, ck_aug, *, batch, d_model):
    n = proj.shape[0]
    t = n // batch
    dh = FOX_HEAD_DIM
    nh = FOX_HEADS_PER_STEP
    groups = d_model // (nh * dh)
    assert ck_aug.shape[1] == groups * LANES
    tq = _tile(t, 512)
    nq = t // tq
    return pl.pallas_call(
        functools.partial(_fox_kernel, tk=tq),
        out_shape=jax.ShapeDtypeStruct((n, d_model), BF16),
        grid=(batch, groups, nq),
        in_specs=[pl.BlockSpec((tq, nh * dh), lambda b, g, i: (b * nq + i, g)),
                  pl.BlockSpec((t, nh * dh), lambda b, g, i: (b, groups + g)),
                  pl.BlockSpec((t, LANES), lambda b, g, i: (b, g)),
                  pl.BlockSpec((t, nh * dh), lambda b, g, i: (b, 2 * groups + g))],
        out_specs=pl.BlockSpec((tq, nh * dh), lambda b, g, i: (b * nq + i, g)),
        scratch_shapes=[pltpu.VMEM((2, nh, tq, tq), F32),
                        pltpu.VMEM((nh, 1, tq), F32),
                        pltpu.VMEM((nh, 1, tq), F32),
                        pltpu.VMEM((nh, dh, tq), F32)],
        compiler_params=_params("parallel", "parallel", "arbitrary"),
        name="fox_attention",
    )(proj, proj, ck_aug, proj)


def _gate_weights_t(w_cols):
    g = w_cols.shape[1]
    return jnp.pad(w_cols.T, ((0, GATE_ROWS - g), (0, 0))).astype(BF16)


def _gate_bias(b):
    return jnp.pad(b.astype(F32), (0, GATE_ROWS - b.shape[0])).reshape(GATE_ROWS, 1)


def kernel(x, c, ada_w, ada_b, norm_mix_w, norm_ffn_w, mlstm_w_in, mlstm_b_gates, mlstm_norm_w, mlstm_w_out,
           fox_w_in, fox_b_f, fox_w_out, ffn_w_gate_up, ffn_w_down, final_norm_w):
    batch, t, d = x.shape
    depth = ada_w.shape[0]
    n = batch * t
    dqk = d // (2 * MLSTM_HEADS)
    mlstm_main = 2 * MLSTM_HEADS * dqk + 2 * d
    fox_main = 3 * d

    c_pad = jnp.pad(c, ((0, GATE_ROWS - batch), (0, 0)))
    mod = _modulation(c_pad, ada_w, ada_b)[:, :batch]
    mod = mod.reshape(depth, batch, 6, 1, d)

    mlstm_w_in_b, mlstm_w_out_b = mlstm_w_in.astype(BF16), mlstm_w_out.astype(BF16)
    fox_w_in_b, fox_w_out_b = fox_w_in.astype(BF16), fox_w_out.astype(BF16)
    ffn_w_gate_up_b, ffn_w_down_b = ffn_w_gate_up.astype(BF16), ffn_w_down.astype(BF16)

    xf = x.reshape(n, d)
    final_w = final_norm_w.reshape(1, d)
    for layer in range(depth):
        sh1, sc1, g1, sh2, sc2, g2 = [mod[layer, :, i] for i in range(6)]
        j = layer // 2
        nw = norm_mix_w[layer].reshape(1, d)
        if layer % 2 == 0:
            proj, gates_t = _inproj(xf, nw, sc1, sh1, mlstm_w_in_b, j, mlstm_main,
                                    _gate_weights_t(mlstm_w_in[j, :, mlstm_main:]), batch=batch,
                                    q_cols=MLSTM_HEADS * dqk, q_scale=dqk ** -0.5, gates_head_major=True)
            y = _mlstm(proj, gates_t, _gate_bias(mlstm_b_gates[j]), mlstm_norm_w[j].reshape(1, d),
                       batch=batch, d_model=d)
            xf = _outproj(y, mlstm_w_out_b, j, xf, g1, batch=batch)
        else:
            heads = d // FOX_HEAD_DIM
            wg = jnp.pad(fox_w_in[j, :, fox_main:], ((0, 0), (0, LANES - heads))).astype(BF16)
            proj, gates = _inproj(xf, nw, sc1, sh1, fox_w_in_b, j, fox_main, wg, batch=batch,
                                  q_cols=d, q_scale=FOX_HEAD_DIM ** -0.5 * LOG2E, gates_head_major=False)
            bias = jnp.pad(fox_b_f[j].astype(F32), (0, LANES - heads)).reshape(1, LANES)
            ck_aug = _cumgate(gates, bias, batch=batch, heads=heads, group=FOX_HEADS_PER_STEP)
            y = _fox_attention(proj, ck_aug, batch=batch, d_model=d)
            xf = _outproj(y, fox_w_out_b, j, xf, g1, batch=batch)
        xf = _ffn(xf, norm_ffn_w[layer].reshape(1, d), sc2, sh2, g2, ffn_w_gate_up_b, ffn_w_down_b, layer,
                  final_w, batch=batch, final_norm=(layer == depth - 1))
    return xf.reshape(batch, t, d)
```

```python
import functools

import jax
import jax.numpy as jnp
from jax import lax
from jax.experimental import pallas as pl
from jax.experimental.pallas import tpu as pltpu

F32 = jnp.float32
BF16 = jnp.bfloat16

EPS = 1e-6
GATE_SOFTCAP = 15.0
MLSTM_HEADS = 4
FOX_HEAD_DIM = 128
GATE_ROWS = 16
LANES = 128
NEG_BIG = -1e30
LOG2E = 1.4426950408889634

VMEM_LIMIT = 56 * 1024 * 1024


def _params(*sem):
    return pltpu.CompilerParams(dimension_semantics=sem, vmem_limit_bytes=VMEM_LIMIT)


def _tile(n, pref):
    t = min(n, pref)
    assert n % t == 0, (n, pref)
    return t


def _log_sigmoid(x):
    return jnp.minimum(x, 0.0) - jnp.log1p(jnp.exp(-jnp.abs(x)))


def _sigmoid(x):
    return 1.0 / (1.0 + jnp.exp(-x))


def _dot(a, b):
    return jnp.dot(a, b, preferred_element_type=F32)


def _dot_nt(a, b):
    return lax.dot_general(a, b, (((1,), (1,)), ((), ())), preferred_element_type=F32)


def _dot_tn(a, b):
    return lax.dot_general(a, b, (((0,), (0,)), ((), ())), preferred_element_type=F32)


def _modulated_norm(x, nw, sc, sh):
    y = x * lax.rsqrt(jnp.mean(x * x, axis=-1, keepdims=True) + EPS)
    return (y * nw) * (1.0 + sc) + sh


def _mod_kernel(c_ref, w_ref, b_ref, o_ref):
    c = c_ref[...]
    cs = (c * _sigmoid(c)).astype(BF16)
    o_ref[0] = _dot(cs, w_ref[0].astype(BF16)) + b_ref[0]


def _modulation(c_pad, ada_w, ada_b):
    depth, d, n6 = ada_w.shape
    rows = c_pad.shape[0]
    tn = _tile(n6, 1024)
    return pl.pallas_call(
        _mod_kernel,
        out_shape=jax.ShapeDtypeStruct((depth, rows, n6), F32),
        grid=(depth, n6 // tn),
        in_specs=[pl.BlockSpec((rows, d), lambda l, j: (0, 0)),
                  pl.BlockSpec((1, d, tn), lambda l, j: (l, 0, j)),
                  pl.BlockSpec((1, 1, tn), lambda l, j: (l, 0, j))],
        out_specs=pl.BlockSpec((1, rows, tn), lambda l, j: (l, 0, j)),
        compiler_params=_params("parallel", "parallel"),
        name="adaln_modulation",
    )(c_pad, ada_w, ada_b.reshape(depth, 1, n6))


def _inproj_kernel(x_ref, nw_ref, sc_ref, sh_ref, w_ref, wg_ref, o_ref, g_ref, h_ref, *, q_cols, q_scale,
                   gates_head_major):
    j = pl.program_id(1)
    tn = o_ref.shape[1]

    def project(hb):
        col = j * tn + lax.broadcasted_iota(jnp.int32, (1, tn), 1)
        scale = jnp.where(col < q_cols, q_scale, 1.0).astype(F32)
        o_ref[...] = (_dot(hb, w_ref[0]) * scale).astype(o_ref.dtype)

    @pl.when(j == 0)
    def _():
        hb = _modulated_norm(x_ref[...], nw_ref[...], sc_ref[0], sh_ref[0]).astype(BF16)
        h_ref[...] = hb
        project(hb)
        if gates_head_major:
            g_ref[0] = _dot_nt(wg_ref[...], hb)
        else:
            g_ref[...] = _dot(hb, wg_ref[...])

    @pl.when(j > 0)
    def _():
        project(h_ref[...])


def _inproj(x, nw, sc, sh, w, layer, nout, wg, *, batch, q_cols, q_scale, gates_head_major):
    n, d = x.shape
    t = n // batch
    tm = _tile(t, 1024)
    tn = _tile(nout, 2048)
    tpb = t // tm
    kern = functools.partial(_inproj_kernel, q_cols=q_cols, q_scale=q_scale, gates_head_major=gates_head_major)
    if gates_head_major:
        g_shape = jax.ShapeDtypeStruct((batch, GATE_ROWS, t), F32)
        g_spec = pl.BlockSpec((1, GATE_ROWS, tm), lambda i, j: (i // tpb, 0, i % tpb))
    else:
        g_shape = jax.ShapeDtypeStruct((n, LANES), F32)
        g_spec = pl.BlockSpec((tm, LANES), lambda i, j: (i, 0))
    return pl.pallas_call(
        kern,
        out_shape=(jax.ShapeDtypeStruct((n, nout), BF16), g_shape),
        grid=(n // tm, nout // tn),
        in_specs=[pl.BlockSpec((tm, d), lambda i, j: (i, 0)),
                  pl.BlockSpec((1, d), lambda i, j: (0, 0)),
                  pl.BlockSpec((1, 1, d), lambda i, j: (i // tpb, 0, 0)),
                  pl.BlockSpec((1, 1, d), lambda i, j: (i // tpb, 0, 0)),
                  pl.BlockSpec((1, d, tn), lambda i, j: (layer, 0, j)),
                  pl.BlockSpec(wg.shape, lambda i, j: (0, 0))],
        out_specs=(pl.BlockSpec((tm, tn), lambda i, j: (i, j)), g_spec),
        scratch_shapes=[pltpu.VMEM((tm, d), BF16)],
        compiler_params=_params("parallel", "arbitrary"),
        name="norm_inproj",
    )(x, nw, sc, sh, w, wg)


def _outproj_kernel(y_ref, w_ref, x_ref, g_ref, o_ref):
    o_ref[...] = x_ref[...] + g_ref[0] * _dot(y_ref[...], w_ref[0])


def _outproj(y, w, layer, x, gate, *, batch):
    n, kdim = y.shape
    d = w.shape[2]
    t = n // batch
    tm = _tile(t, 512)
    tn = d
    tpb = t // tm
    return pl.pallas_call(
        _outproj_kernel,
        out_shape=jax.ShapeDtypeStruct((n, d), F32),
        grid=(n // tm, d // tn),
        in_specs=[pl.BlockSpec((tm, kdim), lambda i, j: (i, 0)),
                  pl.BlockSpec((1, kdim, tn), lambda i, j: (layer, 0, j)),
                  pl.BlockSpec((tm, tn), lambda i, j: (i, j)),
                  pl.BlockSpec((1, 1, tn), lambda i, j: (i // tpb, 0, j))],
        out_specs=pl.BlockSpec((tm, tn), lambda i, j: (i, j)),
        compiler_params=_params("parallel", "arbitrary"),
        name="outproj_residual",
    )(y, w, x, gate)


def _ffn_kernel(x_ref, nw_ref, sc_ref, sh_ref, g2_ref, wg_ref, wu_ref, wd_ref, fw_ref, o_ref, h_ref, *, final_norm):
    f = pl.program_id(1)

    def contribution(h):
        g = _dot(h, wg_ref[0])
        u = _dot(h, wu_ref[0])
        a = (g * _sigmoid(g) * u).astype(BF16)
        return g2_ref[0] * _dot(a, wd_ref[0])

    @pl.when(f == 0)
    def _():
        x = x_ref[...]
        hb = _modulated_norm(x, nw_ref[...], sc_ref[0], sh_ref[0]).astype(BF16)
        h_ref[...] = hb
        o_ref[...] = x + contribution(hb)

    @pl.when(f > 0)
    def _():
        o_ref[...] += contribution(h_ref[...])

    if final_norm:
        @pl.when(f == pl.num_programs(1) - 1)
        def _():
            out = o_ref[...]
            o_ref[...] = out * lax.rsqrt(jnp.mean(out * out, axis=-1, keepdims=True) + EPS) * fw_ref[...]


def _ffn(x, nw, sc, sh, g2, w_gate_up, w_down, layer, final_w, *, batch, final_norm):
    n, d = x.shape
    dff = w_down.shape[1]
    t = n // batch
    tm = _tile(t, 1024)
    tf = _tile(dff, 512)
    tpb = t // tm
    nf = dff // tf
    vec = lambda: pl.BlockSpec((1, 1, d), lambda i, f: (i // tpb, 0, 0))
    return pl.pallas_call(
        functools.partial(_ffn_kernel, final_norm=final_norm),
        out_shape=jax.ShapeDtypeStruct((n, d), F32),
        grid=(n // tm, nf),
        in_specs=[pl.BlockSpec((tm, d), lambda i, f: (i, 0)),
                  pl.BlockSpec((1, d), lambda i, f: (0, 0)),
                  vec(), vec(), vec(),
                  pl.BlockSpec((1, d, tf), lambda i, f: (layer, 0, f)),
                  pl.BlockSpec((1, d, tf), lambda i, f: (layer, 0, nf + f)),
                  pl.BlockSpec((1, tf, d), lambda i, f: (layer, f, 0)),
                  pl.BlockSpec((1, d), lambda i, f: (0, 0))],
        out_specs=pl.BlockSpec((tm, d), lambda i, f: (i, 0)),
        scratch_shapes=[pltpu.VMEM((tm, d), BF16)],
        compiler_params=_params("parallel", "arbitrary"),
        name="swiglu_ffn",
    )(x, nw, sc, sh, g2, w_gate_up, w_gate_up, w_down, final_w)


def _mlstm_kernel(q_ref, k_ref, v_ref, o_ref, g_ref, bias_ref, nw_ref, y_ref, c_ref, n_ref, m_ref, *, heads):
    ci = pl.program_id(1)
    L = q_ref.shape[0]
    dqk = q_ref.shape[1] // heads
    dv = v_ref.shape[1] // heads

    @pl.when(ci == 0)
    def _():
        c_ref[...] = jnp.zeros_like(c_ref)
        n_ref[...] = jnp.zeros_like(n_ref)
        m_ref[...] = jnp.zeros_like(m_ref)

    ones = jnp.ones((L, LANES), BF16)
    q = [q_ref[:, h * dqk:(h + 1) * dqk] for h in range(heads)]
    k = [k_ref[:, h * dqk:(h + 1) * dqk] for h in range(heads)]
    qk = [_dot_nt(q[h], k[h]) for h in range(heads)]
    qc = [_dot(q[h], c_ref[h].astype(BF16)) for h in range(heads)]
    qn = [_dot(q[h], n_ref[h].astype(BF16)) for h in range(heads)]

    gates = g_ref[0] + bias_ref[...]
    gates = GATE_SOFTCAP * jnp.tanh(gates / GATE_SOFTCAP)
    row = lax.broadcasted_iota(jnp.int32, (L, L), 0)
    col = lax.broadcasted_iota(jnp.int32, (L, L), 1)
    causal = row >= col
    eye = row == col

    for h in range(heads):
        li_row = gates[h:h + 1, :]
        lf_row = _log_sigmoid(gates[heads + h:heads + h + 1, :])
        b_col = jnp.sum(jnp.where(causal, lf_row, 0.0), axis=1, keepdims=True)
        b_row = jnp.sum(jnp.where(eye, b_col, 0.0), axis=0, keepdims=True)
        li_col = jnp.sum(jnp.where(eye, li_row, 0.0), axis=1, keepdims=True)
        g_tot = jnp.sum(lf_row, axis=1, keepdims=True)

        m_prev = m_ref[h:h + 1, 0:1]
        dmat = jnp.where(causal, b_col + (li_row - b_row), -jnp.inf)
        inter_log = b_col + m_prev
        m_t = jnp.maximum(inter_log, jnp.max(dmat, axis=1, keepdims=True))
        w_intra = jnp.exp(dmat - m_t)
        w_inter = jnp.exp(inter_log - m_t)

        v = v_ref[:, h * dv:(h + 1) * dv]
        s = (qk[h] * w_intra).astype(BF16)
        num = w_inter * qc[h] + _dot(s, v)
        den = w_inter * qn[h][:, 0:1] + _dot(s, ones)[:, 0:1]
        r = 1.0 / jnp.maximum(jnp.abs(den), jnp.exp(-m_t))
        rms_h = r * jnp.sqrt(jnp.mean(num * num, axis=-1, keepdims=True))
        row_scale = r * lax.rsqrt(rms_h * rms_h + EPS)

        tail = g_tot - b_col + li_col
        m_new = jnp.maximum(g_tot + m_prev, jnp.max(tail, axis=0, keepdims=True))
        w_tail = jnp.exp(tail - m_new)
        decay = jnp.exp(g_tot + m_prev - m_new)
        kw = (k[h].astype(F32) * w_tail).astype(BF16)
        c_ref[h] = decay * c_ref[h] + _dot_tn(kw, v)
        n_ref[h] = decay * n_ref[h] + _dot_tn(kw, ones)
        m_ref[h:h + 1, :] = jnp.broadcast_to(m_new, (1, LANES))

        hs = num * row_scale * nw_ref[:, h * dv:(h + 1) * dv]
        gate = 0.5 + 0.5 * jnp.tanh(0.5 * o_ref[:, h * dv:(h + 1) * dv].astype(F32))
        y_ref[:, h * dv:(h + 1) * dv] = (gate * hs).astype(y_ref.dtype)


def _mlstm(proj, gates_t, bias, norm_w, *, batch, d_model):
    n = proj.shape[0]
    t = n // batch
    heads = MLSTM_HEADS
    dqk = d_model // (2 * heads)
    dv = d_model // heads
    L = _tile(t, 256)
    nc = t // L
    qk_w = heads * dqk
    assert (2 * qk_w) % (heads * dv) == 0 and heads <= 8
    v_blk = (2 * qk_w) // (heads * dv)
    return pl.pallas_call(
        functools.partial(_mlstm_kernel, heads=heads),
        out_shape=jax.ShapeDtypeStruct((n, heads * dv), BF16),
        grid=(batch, nc),
        in_specs=[pl.BlockSpec((L, qk_w), lambda b, c: (b * nc + c, 0)),
                  pl.BlockSpec((L, qk_w), lambda b, c: (b * nc + c, 1)),
                  pl.BlockSpec((L, heads * dv), lambda b, c: (b * nc + c, v_blk)),
                  pl.BlockSpec((L, heads * dv), lambda b, c: (b * nc + c, v_blk + 1)),
                  pl.BlockSpec((1, GATE_ROWS, L), lambda b, c: (b, 0, c)),
                  pl.BlockSpec((GATE_ROWS, 1), lambda b, c: (0, 0)),
                  pl.BlockSpec((1, heads * dv), lambda b, c: (0, 0))],
        out_specs=pl.BlockSpec((L, heads * dv), lambda b, c: (b * nc + c, 0)),
        scratch_shapes=[pltpu.VMEM((heads, dqk, dv), F32), pltpu.VMEM((heads, dqk, LANES), F32),
                        pltpu.VMEM((8, LANES), F32)],
        compiler_params=_params("parallel", "arbitrary"),
        name="mlstm_chunkwise",
    )(proj, proj, proj, proj, gates_t, bias, norm_w)


BIAS_PIECES = 3


def _split_bf16(x):
    p0 = x.astype(BF16)
    r1 = x - p0.astype(F32)
    p1 = r1.astype(BF16)
    p2 = (r1 - p1.astype(F32)).astype(BF16)
    return p0, p1, p2


def _cumgate_kernel(g_ref, bias_ref, o_ref, carry_ref, *, group):
    @pl.when(pl.program_id(1) == 0)
    def _():
        carry_ref[...] = jnp.zeros_like(carry_ref)

    lf = _log_sigmoid(g_ref[...] + bias_ref[...])
    tc = lf.shape[0]
    row = lax.broadcasted_iota(jnp.int32, (tc, tc), 0)
    col = lax.broadcasted_iota(jnp.int32, (tc, tc), 1)
    lower = jnp.where(row >= col, 1.0, 0.0).astype(BF16)
    cum = carry_ref[0:1, :]
    for piece in _split_bf16(lf):
        cum = cum + _dot(lower, piece)
    carry_ref[...] = jnp.broadcast_to(cum[tc - 1:tc, :], carry_ref.shape)

    width = o_ref.shape[1]
    src = lax.broadcasted_iota(jnp.int32, (LANES, width), 0)
    dst = lax.broadcasted_iota(jnp.int32, (LANES, width), 1)
    dst_head = (dst // LANES) * group + dst % group
    dst_piece = (dst % LANES) // group
    out = jnp.zeros((tc, width), F32)
    for j, piece in enumerate(_split_bf16(cum * LOG2E)):
        sel = jnp.where((dst_head == src) & (dst_piece == j), 1.0, 0.0).astype(BF16)
        out = out + _dot(piece, sel)
    o_ref[...] = out.astype(o_ref.dtype)


def _cumgate(gates, bias, *, batch, heads, group):
    n, lanes = gates.shape
    t = n // batch
    tc = _tile(t, 512)
    nc = t // tc
    assert heads % group == 0 and group & (group - 1) == 0 and BIAS_PIECES * group <= LANES
    width = (heads // group) * LANES
    return pl.pallas_call(
        functools.partial(_cumgate_kernel, group=group),
        out_shape=jax.ShapeDtypeStruct((n, width), BF16),
        grid=(batch, nc),
        in_specs=[pl.BlockSpec((tc, lanes), lambda b, c: (b * nc + c, 0)),
                  pl.BlockSpec((1, lanes), lambda b, c: (0, 0))],
        out_specs=pl.BlockSpec((tc, width), lambda b, c: (b * nc + c, 0)),
        scratch_shapes=[pltpu.VMEM((8, lanes), F32)],
        compiler_params=_params("parallel", "arbitrary"),
        name="fox_cum_log_forget",
    )(gates, bias)


def _fox_kernel(q_ref, k_ref, ck_ref, v_ref, o_ref, s_ref, m_ref, l_ref, acc_ref, *, tk):
    qi = pl.program_id(2)
    tq = q_ref.shape[0]
    dh = FOX_HEAD_DIM
    nh = q_ref.shape[1] // dh
    assert tq == tk
    lane = lax.broadcasted_iota(jnp.int32, (tq, LANES), 1)
    q_aug = []
    for hd in range(nh):
        minus_one = jnp.where((lane % nh == hd) & (lane < BIAS_PIECES * nh), -1.0, 0.0).astype(BF16)
        q_aug.append(jnp.concatenate([q_ref[:, hd * dh:(hd + 1) * dh], minus_one], axis=1))

    def head_scores(hd, ki, slot):
        start = pl.multiple_of(ki * tk, tk)
        k_aug = jnp.concatenate([k_ref[pl.ds(start, tk), hd * dh:(hd + 1) * dh],
                                 ck_ref[pl.ds(start, tk), :]], axis=1)
        s_ref[slot, hd] = _dot_nt(k_aug, q_aug[hd])

    def head_softmax_pv(hd, ki, slot, keep, first):
        start = pl.multiple_of(ki * tk, tk)
        v = v_ref[pl.ds(start, tk), hd * dh:(hd + 1) * dh]
        for c0 in range(0, tq, QCHUNK):
            qs = slice(c0, c0 + QCHUNK)
            s = s_ref[slot, hd, :, qs]
            if keep is not None:
                s = jnp.where(keep[:, qs], s, -jnp.inf)
            m_blk = jnp.max(s, axis=0, keepdims=True)
            m_new = m_blk if first else jnp.maximum(m_ref[hd, :, qs], m_blk)
            p = jnp.exp2(s - m_new)
            l_blk = jnp.sum(p, axis=0, keepdims=True)
            pv = _dot_tn(v, p.astype(BF16))
            if first:
                l_ref[hd, :, qs] = l_blk
                acc_ref[hd, :, qs] = pv
            else:
                alpha = jnp.exp2(m_ref[hd, :, qs] - m_new)
                l_ref[hd, :, qs] = alpha * l_ref[hd, :, qs] + l_blk
                acc_ref[hd, :, qs] = alpha * acc_ref[hd, :, qs] + pv
            m_ref[hd, :, qs] = m_new

    def scores(ki, slot):
        for hd in range(nh):
            head_scores(hd, ki, slot)

    def softmax_pv(ki, slot, keep=None, next_block=None, first=False):
        for hd in range(nh):
            if next_block is not None:
                head_scores(hd, *next_block)
            head_softmax_pv(hd, ki, slot, keep, first)

    key = lax.broadcasted_iota(jnp.int32, (tk, tq), 0)
    qry = lax.broadcasted_iota(jnp.int32, (tk, tq), 1)
    scores(qi, 0)
    softmax_pv(qi, 0, keep=key <= qry, next_block=(0, 1), first=True)

    def pair(j, carry):
        softmax_pv(2 * j, 1, next_block=(2 * j + 1, 0))
        softmax_pv(2 * j + 1, 0, next_block=(jnp.minimum(2 * j + 2, qi), 1))
        return carry

    lax.fori_loop(0, qi // 2, pair, 0)

    @pl.when(qi % 2 == 1)
    def _():
        softmax_pv(qi - 1, 1)

    for hd in range(nh):
        o_ref[:, hd * dh:(hd + 1) * dh] = (acc_ref[hd] * (1.0 / l_ref[hd])).T.astype(o_ref.dtype)


FOX_HEADS_PER_STEP = 4
QCHUNK = 256


def _fox_attention(proj, ck_aug, *, batch, d_model):
    n = proj.shape[0]
    t = n // batch
    dh = FOX_HEAD_DIM
    nh = FOX_HEADS_PER_STEP
    groups = d_model // (nh * dh)
    assert ck_aug.shape[1] == groups * LANES
    tq = _tile(t, 512)
    nq = t // tq
    return pl.pallas_call(
        functools.partial(_fox_kernel, tk=tq),
        out_shape=jax.ShapeDtypeStruct((n, d_model), BF16),
        grid=(batch, groups, nq),
        in_specs=[pl.BlockSpec((tq, nh * dh), lambda b, g, i: (b * nq + i, g)),
                  pl.BlockSpec((t, nh * dh), lambda b, g, i: (b, groups + g)),
                  pl.BlockSpec((t, LANES), lambda b, g, i: (b, g)),
                  pl.BlockSpec((t, nh * dh), lambda b, g, i: (b, 2 * groups + g))],
        out_specs=pl.BlockSpec((tq, nh * dh), lambda b, g, i: (b * nq + i, g)),
        scratch_shapes=[pltpu.VMEM((2, nh, tq, tq), F32),
                        pltpu.VMEM((nh, 1, tq), F32),
                        pltpu.VMEM((nh, 1, tq), F32),
                        pltpu.VMEM((nh, dh, tq), F32)],
        compiler_params=_params("parallel", "parallel", "arbitrary"),
        name="fox_attention",
    )(proj, proj, ck_aug, proj)


def _gate_weights_t(w_cols):
    g = w_cols.shape[1]
    return jnp.pad(w_cols.T, ((0, GATE_ROWS - g), (0, 0))).astype(BF16)


def _gate_bias(b):
    return jnp.pad(b.astype(F32), (0, GATE_ROWS - b.shape[0])).reshape(GATE_ROWS, 1)


def kernel(x, c, ada_w, ada_b, norm_mix_w, norm_ffn_w, mlstm_w_in, mlstm_b_gates, mlstm_norm_w, mlstm_w_out,
           fox_w_in, fox_b_f, fox_w_out, ffn_w_gate_up, ffn_w_down, final_norm_w):
    batch, t, d = x.shape
    depth = ada_w.shape[0]
    n = batch * t
    dqk = d // (2 * MLSTM_HEADS)
    mlstm_main = 2 * MLSTM_HEADS * dqk + 2 * d
    fox_main = 3 * d

    c_pad = jnp.pad(c, ((0, GATE_ROWS - batch), (0, 0)))
    mod = _modulation(c_pad, ada_w, ada_b)[:, :batch]
    mod = mod.reshape(depth, batch, 6, 1, d)

    mlstm_w_in_b, mlstm_w_out_b = mlstm_w_in.astype(BF16), mlstm_w_out.astype(BF16)
    fox_w_in_b, fox_w_out_b = fox_w_in.astype(BF16), fox_w_out.astype(BF16)
    ffn_w_gate_up_b, ffn_w_down_b = ffn_w_gate_up.astype(BF16), ffn_w_down.astype(BF16)

    xf = x.reshape(n, d)
    final_w = final_norm_w.reshape(1, d)
    for layer in range(depth):
        sh1, sc1, g1, sh2, sc2, g2 = [mod[layer, :, i] for i in range(6)]
        j = layer // 2
        nw = norm_mix_w[layer].reshape(1, d)
        if layer % 2 == 0:
            proj, gates_t = _inproj(xf, nw, sc1, sh1, mlstm_w_in_b, j, mlstm_main,
                                    _gate_weights_t(mlstm_w_in[j, :, mlstm_main:]), batch=batch,
                                    q_cols=MLSTM_HEADS * dqk, q_scale=dqk ** -0.5, gates_head_major=True)
            y = _mlstm(proj, gates_t, _gate_bias(mlstm_b_gates[j]), mlstm_norm_w[j].reshape(1, d),
                       batch=batch, d_model=d)
            xf = _outproj(y, mlstm_w_out_b, j, xf, g1, batch=batch)
        else:
            heads = d // FOX_HEAD_DIM
            wg = jnp.pad(fox_w_in[j, :, fox_main:], ((0, 0), (0, LANES - heads))).astype(BF16)
            proj, gates = _inproj(xf, nw, sc1, sh1, fox_w_in_b, j, fox_main, wg, batch=batch,
                                  q_cols=d, q_scale=FOX_HEAD_DIM ** -0.5 * LOG2E, gates_head_major=False)
            bias = jnp.pad(fox_b_f[j].astype(F32), (0, LANES - heads)).reshape(1, LANES)
            ck_aug = _cumgate(gates, bias, batch=batch, heads=heads, group=FOX_HEADS_PER_STEP)
            y = _fox_attention(proj, ck_aug, batch=batch, d_model=d)
            xf = _outproj(y, fox_w_out_b, j, xf, g1, batch=batch)
        xf = _ffn(xf, norm_ffn_w[layer].reshape(1, d), sc2, sh2, g2, ffn_w_gate_up_b, ffn_w_down_b, layer,
                  final_w, batch=batch, final_norm=(layer == depth - 1))
    return xf.reshape(batch, t, d)
```

```python
import functools

import jax
import jax.numpy as jnp
from jax import lax
from jax.experimental import pallas as pl
from jax.experimental.pallas import tpu as pltpu

F32 = jnp.float32
BF16 = jnp.bfloat16

EPS = 1e-6
GATE_SOFTCAP = 15.0
MLSTM_HEADS = 4
FOX_HEAD_DIM = 128
GATE_ROWS = 16
LANES = 128
NEG_BIG = -1e30
LOG2E = 1.4426950408889634

VMEM_LIMIT = 56 * 1024 * 1024


def _params(*sem):
    return pltpu.CompilerParams(dimension_semantics=sem, vmem_limit_bytes=VMEM_LIMIT)


def _tile(n, pref):
    t = min(n, pref)
    assert n % t == 0, (n, pref)
    return t


def _log_sigmoid(x):
    return jnp.minimum(x, 0.0) - jnp.log1p(jnp.exp(-jnp.abs(x)))


def _sigmoid(x):
    return 1.0 / (1.0 + jnp.exp(-x))


def _dot(a, b):
    return jnp.dot(a, b, preferred_element_type=F32)


def _dot_nt(a, b):
    return lax.dot_general(a, b, (((1,), (1,)), ((), ())), preferred_element_type=F32)


def _dot_tn(a, b):
    return lax.dot_general(a, b, (((0,), (0,)), ((), ())), preferred_element_type=F32)


def _modulated_norm(x, nw, sc, sh):
    y = x * lax.rsqrt(jnp.mean(x * x, axis=-1, keepdims=True) + EPS)
    return (y * nw) * (1.0 + sc) + sh


def _mod_kernel(c_ref, w_ref, b_ref, o_ref):
    c = c_ref[...]
    cs = (c * _sigmoid(c)).astype(BF16)
    o_ref[0] = _dot(cs, w_ref[0].astype(BF16)) + b_ref[0]


def _modulation(c_pad, ada_w, ada_b):
    depth, d, n6 = ada_w.shape
    rows = c_pad.shape[0]
    tn = _tile(n6, 1024)
    return pl.pallas_call(
        _mod_kernel,
        out_shape=jax.ShapeDtypeStruct((depth, rows, n6), F32),
        grid=(depth, n6 // tn),
        in_specs=[pl.BlockSpec((rows, d), lambda l, j: (0, 0)),
                  pl.BlockSpec((1, d, tn), lambda l, j: (l, 0, j)),
                  pl.BlockSpec((1, 1, tn), lambda l, j: (l, 0, j))],
        out_specs=pl.BlockSpec((1, rows, tn), lambda l, j: (l, 0, j)),
        compiler_params=_params("parallel", "parallel"),
        name="adaln_modulation",
    )(c_pad, ada_w, ada_b.reshape(depth, 1, n6))


def _inproj_kernel(x_ref, nw_ref, sc_ref, sh_ref, w_ref, wg_ref, o_ref, g_ref, h_ref, *, q_cols, q_scale,
                   gates_head_major):
    j = pl.program_id(1)
    tn = o_ref.shape[1]

    def project(hb):
        col = j * tn + lax.broadcasted_iota(jnp.int32, (1, tn), 1)
        scale = jnp.where(col < q_cols, q_scale, 1.0).astype(F32)
        o_ref[...] = (_dot(hb, w_ref[0]) * scale).astype(o_ref.dtype)

    @pl.when(j == 0)
    def _():
        hb = _modulated_norm(x_ref[...], nw_ref[...], sc_ref[0], sh_ref[0]).astype(BF16)
        h_ref[...] = hb
        project(hb)
        if gates_head_major:
            g_ref[0] = _dot_nt(wg_ref[...], hb)
        else:
            g_ref[...] = _dot(hb, wg_ref[...])

    @pl.when(j > 0)
    def _():
        project(h_ref[...])


def _inproj(x, nw, sc, sh, w, layer, nout, wg, *, batch, q_cols, q_scale, gates_head_major):
    n, d = x.shape
    t = n // batch
    tm = _tile(t, 1024)
    tn = _tile(nout, 2048)
    tpb = t // tm
    kern = functools.partial(_inproj_kernel, q_cols=q_cols, q_scale=q_scale, gates_head_major=gates_head_major)
    if gates_head_major:
        g_shape = jax.ShapeDtypeStruct((batch, GATE_ROWS, t), F32)
        g_spec = pl.BlockSpec((1, GATE_ROWS, tm), lambda i, j: (i // tpb, 0, i % tpb))
    else:
        g_shape = jax.ShapeDtypeStruct((n, LANES), F32)
        g_spec = pl.BlockSpec((tm, LANES), lambda i, j: (i, 0))
    return pl.pallas_call(
        kern,
        out_shape=(jax.ShapeDtypeStruct((n, nout), BF16), g_shape),
        grid=(n // tm, nout // tn),
        in_specs=[pl.BlockSpec((tm, d), lambda i, j: (i, 0)),
                  pl.BlockSpec((1, d), lambda i, j: (0, 0)),
                  pl.BlockSpec((1, 1, d), lambda i, j: (i // tpb, 0, 0)),
                  pl.BlockSpec((1, 1, d), lambda i, j: (i // tpb, 0, 0)),
                  pl.BlockSpec((1, d, tn), lambda i, j: (layer, 0, j)),
                  pl.BlockSpec(wg.shape, lambda i, j: (0, 0))],
        out_specs=(pl.BlockSpec((tm, tn), lambda i, j: (i, j)), g_spec),
        scratch_shapes=[pltpu.VMEM((tm, d), BF16)],
        compiler_params=_params("parallel", "arbitrary"),
        name="norm_inproj",
    )(x, nw, sc, sh, w, wg)


def _outproj_kernel(y_ref, w_ref, x_ref, g_ref, o_ref):
    o_ref[...] = x_ref[...] + g_ref[0] * _dot(y_ref[...], w_ref[0])


def _outproj(y, w, layer, x, gate, *, batch):
    n, kdim = y.shape
    d = w.shape[2]
    t = n // batch
    tm = _tile(t, 512)
    tn = d
    tpb = t // tm
    return pl.pallas_call(
        _outproj_kernel,
        out_shape=jax.ShapeDtypeStruct((n, d), F32),
        grid=(n // tm, d // tn),
        in_specs=[pl.BlockSpec((tm, kdim), lambda i, j: (i, 0)),
                  pl.BlockSpec((1, kdim, tn), lambda i, j: (layer, 0, j)),
                  pl.BlockSpec((tm, tn), lambda i, j: (i, j)),
                  pl.BlockSpec((1, 1, tn), lambda i, j: (i // tpb, 0, j))],
        out_specs=pl.BlockSpec((tm, tn), lambda i, j: (i, j)),
        compiler_params=_params("parallel", "arbitrary"),
        name="outproj_residual",
    )(y, w, x, gate)


def _ffn_kernel(x_ref, nw_ref, sc_ref, sh_ref, g2_ref, wg_ref, wu_ref, wd_ref, fw_ref, o_ref, h_ref, *, final_norm):
    f = pl.program_id(1)

    def contribution(h):
        g = _dot(h, wg_ref[0])
        u = _dot(h, wu_ref[0])
        a = (g * _sigmoid(g) * u).astype(BF16)
        return g2_ref[0] * _dot(a, wd_ref[0])

    @pl.when(f == 0)
    def _():
        x = x_ref[...]
        hb = _modulated_norm(x, nw_ref[...], sc_ref[0], sh_ref[0]).astype(BF16)
        h_ref[...] = hb
        o_ref[...] = x + contribution(hb)

    @pl.when(f > 0)
    def _():
        o_ref[...] += contribution(h_ref[...])

    if final_norm:
        @pl.when(f == pl.num_programs(1) - 1)
        def _():
            out = o_ref[...]
            o_ref[...] = out * lax.rsqrt(jnp.mean(out * out, axis=-1, keepdims=True) + EPS) * fw_ref[...]


def _ffn(x, nw, sc, sh, g2, w_gate_up, w_down, layer, final_w, *, batch, final_norm):
    n, d = x.shape
    dff = w_down.shape[1]
    t = n // batch
    tm = _tile(t, 1024)
    tf = _tile(dff, 512)
    tpb = t // tm
    nf = dff // tf
    vec = lambda: pl.BlockSpec((1, 1, d), lambda i, f: (i // tpb, 0, 0))
    return pl.pallas_call(
        functools.partial(_ffn_kernel, final_norm=final_norm),
        out_shape=jax.ShapeDtypeStruct((n, d), F32),
        grid=(n // tm, nf),
        in_specs=[pl.BlockSpec((tm, d), lambda i, f: (i, 0)),
                  pl.BlockSpec((1, d), lambda i, f: (0, 0)),
                  vec(), vec(), vec(),
                  pl.BlockSpec((1, d, tf), lambda i, f: (layer, 0, f)),
                  pl.BlockSpec((1, d, tf), lambda i, f: (layer, 0, nf + f)),
                  pl.BlockSpec((1, tf, d), lambda i, f: (layer, f, 0)),
                  pl.BlockSpec((1, d), lambda i, f: (0, 0))],
        out_specs=pl.BlockSpec((tm, d), lambda i, f: (i, 0)),
        scratch_shapes=[pltpu.VMEM((tm, d), BF16)],
        compiler_params=_params("parallel", "arbitrary"),
        name="swiglu_ffn",
    )(x, nw, sc, sh, g2, w_gate_up, w_gate_up, w_down, final_w)


def _mlstm_kernel(q_ref, k_ref, v_ref, o_ref, g_ref, bias_ref, nw_ref, y_ref, c_ref, n_ref, m_ref, *, heads):
    ci = pl.program_id(1)
    L = q_ref.shape[0]
    dqk = q_ref.shape[1] // heads
    dv = v_ref.shape[1] // heads

    @pl.when(ci == 0)
    def _():
        c_ref[...] = jnp.zeros_like(c_ref)
        n_ref[...] = jnp.zeros_like(n_ref)
        m_ref[...] = jnp.zeros_like(m_ref)

    ones = jnp.ones((L, LANES), BF16)
    ones_dv = jnp.ones((dv, LANES), BF16)
    q = [q_ref[:, h * dqk:(h + 1) * dqk] for h in range(heads)]
    k = [k_ref[:, h * dqk:(h + 1) * dqk] for h in range(heads)]
    qk = [_dot_nt(q[h], k[h]) for h in range(heads)]
    qc = [_dot(q[h], c_ref[h].astype(BF16)) for h in range(heads)]
    qn = [_dot(q[h], n_ref[h].astype(BF16)) for h in range(heads)]

    gates = g_ref[0] + bias_ref[...]
    gates = GATE_SOFTCAP * jnp.tanh(gates / GATE_SOFTCAP)
    row = lax.broadcasted_iota(jnp.int32, (L, L), 0)
    col = lax.broadcasted_iota(jnp.int32, (L, L), 1)
    causal = row >= col
    eye = row == col

    for h in range(heads):
        li_row = gates[h:h + 1, :]
        lf_row = _log_sigmoid(gates[heads + h:heads + h + 1, :])
        b_col = jnp.sum(jnp.where(causal, lf_row, 0.0), axis=1, keepdims=True)
        b_row = jnp.sum(jnp.where(eye, b_col, 0.0), axis=0, keepdims=True)
        li_col = jnp.sum(jnp.where(eye, li_row, 0.0), axis=1, keepdims=True)
        g_tot = jnp.sum(lf_row, axis=1, keepdims=True)

        m_prev = m_ref[h:h + 1, 0:1]
        dmat = jnp.where(causal, b_col + (li_row - b_row), -jnp.inf)
        inter_log = b_col + m_prev
        m_t = jnp.maximum(inter_log, jnp.max(dmat, axis=1, keepdims=True))
        w_intra = jnp.exp(dmat - m_t)
        w_inter = jnp.exp(inter_log - m_t)

        v = v_ref[:, h * dv:(h + 1) * dv]
        s = (qk[h] * w_intra).astype(BF16)
        num = w_inter * qc[h] + _dot(s, v)
        den = w_inter * qn[h][:, 0:1] + _dot(s, ones)[:, 0:1]
        r = 1.0 / jnp.maximum(jnp.abs(den), jnp.exp(-m_t))
        sumsq = _dot((num * num).astype(BF16), ones_dv)[:, 0:1]
        rms_h = r * jnp.sqrt(sumsq * (1.0 / dv))
        row_scale = r * lax.rsqrt(rms_h * rms_h + EPS)

        tail = g_tot - b_col + li_col
        m_new = jnp.maximum(g_tot + m_prev, jnp.max(tail, axis=0, keepdims=True))
        w_tail = jnp.exp(tail - m_new)
        decay = jnp.exp(g_tot + m_prev - m_new)
        kw = (k[h].astype(F32) * w_tail).astype(BF16)
        c_ref[h] = decay * c_ref[h] + _dot_tn(kw, v)
        n_ref[h] = decay * n_ref[h] + _dot_tn(kw, ones)
        m_ref[h:h + 1, :] = jnp.broadcast_to(m_new, (1, LANES))

        half_hs = (num * (0.5 * row_scale) * nw_ref[:, h * dv:(h + 1) * dv]).astype(BF16)
        t = jnp.tanh(o_ref[:, h * dv:(h + 1) * dv] * 0.5)
        y_ref[:, h * dv:(h + 1) * dv] = (half_hs + half_hs * t).astype(y_ref.dtype)


def _mlstm(proj, gates_t, bias, norm_w, *, batch, d_model):
    n = proj.shape[0]
    t = n // batch
    heads = MLSTM_HEADS
    dqk = d_model // (2 * heads)
    dv = d_model // heads
    L = _tile(t, 256)
    nc = t // L
    qk_w = heads * dqk
    assert (2 * qk_w) % (heads * dv) == 0 and heads <= 8
    v_blk = (2 * qk_w) // (heads * dv)
    return pl.pallas_call(
        functools.partial(_mlstm_kernel, heads=heads),
        out_shape=jax.ShapeDtypeStruct((n, heads * dv), BF16),
        grid=(batch, nc),
        in_specs=[pl.BlockSpec((L, qk_w), lambda b, c: (b * nc + c, 0)),
                  pl.BlockSpec((L, qk_w), lambda b, c: (b * nc + c, 1)),
                  pl.BlockSpec((L, heads * dv), lambda b, c: (b * nc + c, v_blk)),
                  pl.BlockSpec((L, heads * dv), lambda b, c: (b * nc + c, v_blk + 1)),
                  pl.BlockSpec((1, GATE_ROWS, L), lambda b, c: (b, 0, c)),
                  pl.BlockSpec((GATE_ROWS, 1), lambda b, c: (0, 0)),
                  pl.BlockSpec((1, heads * dv), lambda b, c: (0, 0))],
        out_specs=pl.BlockSpec((L, heads * dv), lambda b, c: (b * nc + c, 0)),
        scratch_shapes=[pltpu.VMEM((heads, dqk, dv), F32), pltpu.VMEM((heads, dqk, LANES), F32),
                        pltpu.VMEM((8, LANES), F32)],
        compiler_params=_params("parallel", "arbitrary"),
        name="mlstm_chunkwise",
    )(proj, proj, proj, proj, gates_t, bias, norm_w)


BIAS_PIECES = 3


def _split_bf16(x):
    p0 = x.astype(BF16)
    r1 = x - p0.astype(F32)
    p1 = r1.astype(BF16)
    p2 = (r1 - p1.astype(F32)).astype(BF16)
    return p0, p1, p2


def _cumgate_kernel(g_ref, bias_ref, o_ref, carry_ref, *, group):
    @pl.when(pl.program_id(1) == 0)
    def _():
        carry_ref[...] = jnp.zeros_like(carry_ref)

    lf = _log_sigmoid(g_ref[...] + bias_ref[...])
    tc = lf.shape[0]
    row = lax.broadcasted_iota(jnp.int32, (tc, tc), 0)
    col = lax.broadcasted_iota(jnp.int32, (tc, tc), 1)
    lower = jnp.where(row >= col, 1.0, 0.0).astype(BF16)
    cum = carry_ref[0:1, :]
    for piece in _split_bf16(lf):
        cum = cum + _dot(lower, piece)
    carry_ref[...] = jnp.broadcast_to(cum[tc - 1:tc, :], carry_ref.shape)

    width = o_ref.shape[1]
    src = lax.broadcasted_iota(jnp.int32, (LANES, width), 0)
    dst = lax.broadcasted_iota(jnp.int32, (LANES, width), 1)
    dst_head = (dst // LANES) * group + dst % group
    dst_piece = (dst % LANES) // group
    out = jnp.zeros((tc, width), F32)
    for j, piece in enumerate(_split_bf16(cum * LOG2E)):
        sel = jnp.where((dst_head == src) & (dst_piece == j), 1.0, 0.0).astype(BF16)
        out = out + _dot(piece, sel)
    o_ref[...] = out.astype(o_ref.dtype)


def _cumgate(gates, bias, *, batch, heads, group):
    n, lanes = gates.shape
    t = n // batch
    tc = _tile(t, 512)
    nc = t // tc
    assert heads % group == 0 and group & (group - 1) == 0 and BIAS_PIECES * group <= LANES
    width = (heads // group) * LANES
    return pl.pallas_call(
        functools.partial(_cumgate_kernel, group=group),
        out_shape=jax.ShapeDtypeStruct((n, width), BF16),
        grid=(batch, nc),
        in_specs=[pl.BlockSpec((tc, lanes), lambda b, c: (b * nc + c, 0)),
                  pl.BlockSpec((1, lanes), lambda b, c: (0, 0))],
        out_specs=pl.BlockSpec((tc, width), lambda b, c: (b * nc + c, 0)),
        scratch_shapes=[pltpu.VMEM((8, lanes), F32)],
        compiler_params=_params("parallel", "arbitrary"),
        name="fox_cum_log_forget",
    )(gates, bias)


def _fox_kernel(q_ref, k_ref, ck_ref, v_ref, o_ref, s_ref, m_ref, l_ref, acc_ref, *, tk):
    qi = pl.program_id(2)
    tq = q_ref.shape[0]
    dh = FOX_HEAD_DIM
    nh = q_ref.shape[1] // dh
    assert tq == tk
    lane = lax.broadcasted_iota(jnp.int32, (tq, LANES), 1)
    q_aug = []
    for hd in range(nh):
        minus_one = jnp.where((lane % nh == hd) & (lane < BIAS_PIECES * nh), -1.0, 0.0).astype(BF16)
        q_aug.append(jnp.concatenate([q_ref[:, hd * dh:(hd + 1) * dh], minus_one], axis=1))

    def head_scores(hd, ki, slot):
        start = pl.multiple_of(ki * tk, tk)
        k_aug = jnp.concatenate([k_ref[pl.ds(start, tk), hd * dh:(hd + 1) * dh],
                                 ck_ref[pl.ds(start, tk), :]], axis=1)
        s_ref[slot, hd] = _dot_nt(k_aug, q_aug[hd])

    def head_softmax_pv(hd, ki, slot, keep, first):
        start = pl.multiple_of(ki * tk, tk)
        v = v_ref[pl.ds(start, tk), hd * dh:(hd + 1) * dh]
        for c0 in range(0, tq, QCHUNK):
            qs = slice(c0, c0 + QCHUNK)
            s = s_ref[slot, hd, :, qs]
            if keep is not None:
                s = jnp.where(keep[:, qs], s, -jnp.inf)
            m_blk = jnp.max(s, axis=0, keepdims=True)
            m_new = m_blk if first else jnp.maximum(m_ref[hd, :, qs], m_blk)
            p = jnp.exp2(s - m_new)
            l_blk = jnp.sum(p, axis=0, keepdims=True)
            pv = _dot_tn(v, p.astype(BF16))
            if first:
                l_ref[hd, :, qs] = l_blk
                acc_ref[hd, :, qs] = pv
            else:
                alpha = jnp.exp2(m_ref[hd, :, qs] - m_new)
                l_ref[hd, :, qs] = alpha * l_ref[hd, :, qs] + l_blk
                acc_ref[hd, :, qs] = alpha * acc_ref[hd, :, qs] + pv
            m_ref[hd, :, qs] = m_new

    def scores(ki, slot):
        for hd in range(nh):
            head_scores(hd, ki, slot)

    def softmax_pv(ki, slot, keep=None, next_block=None, first=False):
        for hd in range(nh):
            if next_block is not None:
                head_scores(hd, *next_block)
            head_softmax_pv(hd, ki, slot, keep, first)

    key = lax.broadcasted_iota(jnp.int32, (tk, tq), 0)
    qry = lax.broadcasted_iota(jnp.int32, (tk, tq), 1)
    scores(qi, 0)
    softmax_pv(qi, 0, keep=key <= qry, next_block=(0, 1), first=True)

    def pair(j, carry):
        softmax_pv(2 * j, 1, next_block=(2 * j + 1, 0))
        softmax_pv(2 * j + 1, 0, next_block=(2 * j + 2, 1))
        return carry

    lax.fori_loop(0, (qi - 1) // 2, pair, 0)

    @pl.when(qi % 2 == 1)
    def _():
        softmax_pv(qi - 1, 1)

    @pl.when((qi % 2 == 0) & (qi > 0))
    def _():
        softmax_pv(qi - 2, 1, next_block=(qi - 1, 0))
        softmax_pv(qi - 1, 0)

    for hd in range(nh):
        o_ref[:, hd * dh:(hd + 1) * dh] = (acc_ref[hd] * (1.0 / l_ref[hd])).T.astype(o_ref.dtype)


FOX_HEADS_PER_STEP = 4
QCHUNK = 256


def _fox_attention(proj, ck_aug, *, batch, d_model):
    n = proj.shape[0]
    t = n // batch
    dh = FOX_HEAD_DIM
    nh = FOX_HEADS_PER_STEP
    groups = d_model // (nh * dh)
    assert ck_aug.shape[1] == groups * LANES
    tq = _tile(t, 512)
    nq = t // tq
    return pl.pallas_call(
        functools.partial(_fox_kernel, tk=tq),
        out_shape=jax.ShapeDtypeStruct((n, d_model), BF16),
        grid=(batch, groups, nq),
        in_specs=[pl.BlockSpec((tq, nh * dh), lambda b, g, i: (b * nq + i, g)),
                  pl.BlockSpec((t, nh * dh), lambda b, g, i: (b, groups + g)),
                  pl.BlockSpec((t, LANES), lambda b, g, i: (b, g)),
                  pl.BlockSpec((t, nh * dh), lambda b, g, i: (b, 2 * groups + g))],
        out_specs=pl.BlockSpec((tq, nh * dh), lambda b, g, i: (b * nq + i, g)),
        scratch_shapes=[pltpu.VMEM((2, nh, tq, tq), F32),
                        pltpu.VMEM((nh, 1, tq), F32),
                        pltpu.VMEM((nh, 1, tq), F32),
                        pltpu.VMEM((nh, dh, tq), F32)],
        compiler_params=_params("parallel", "parallel", "arbitrary"),
        name="fox_attention",
    )(proj, proj, ck_aug, proj)


def _gate_weights_t(w_cols):
    g = w_cols.shape[1]
    return jnp.pad(w_cols.T, ((0, GATE_ROWS - g), (0, 0))).astype(BF16)


def _gate_bias(b):
    return jnp.pad(b.astype(F32), (0, GATE_ROWS - b.shape[0])).reshape(GATE_ROWS, 1)


def kernel(x, c, ada_w, ada_b, norm_mix_w, norm_ffn_w, mlstm_w_in, mlstm_b_gates, mlstm_norm_w, mlstm_w_out,
           fox_w_in, fox_b_f, fox_w_out, ffn_w_gate_up, ffn_w_down, final_norm_w):
    batch, t, d = x.shape
    depth = ada_w.shape[0]
    n = batch * t
    dqk = d // (2 * MLSTM_HEADS)
    mlstm_main = 2 * MLSTM_HEADS * dqk + 2 * d
    fox_main = 3 * d

    c_pad = jnp.pad(c, ((0, GATE_ROWS - batch), (0, 0)))
    mod = _modulation(c_pad, ada_w, ada_b)[:, :batch]
    mod = mod.reshape(depth, batch, 6, 1, d)

    mlstm_w_in_b, mlstm_w_out_b = mlstm_w_in.astype(BF16), mlstm_w_out.astype(BF16)
    fox_w_in_b, fox_w_out_b = fox_w_in.astype(BF16), fox_w_out.astype(BF16)
    ffn_w_gate_up_b, ffn_w_down_b = ffn_w_gate_up.astype(BF16), ffn_w_down.astype(BF16)

    xf = x.reshape(n, d)
    final_w = final_norm_w.reshape(1, d)
    for layer in range(depth):
        sh1, sc1, g1, sh2, sc2, g2 = [mod[layer, :, i] for i in range(6)]
        j = layer // 2
        nw = norm_mix_w[layer].reshape(1, d)
        if layer % 2 == 0:
            proj, gates_t = _inproj(xf, nw, sc1, sh1, mlstm_w_in_b, j, mlstm_main,
                                    _gate_weights_t(mlstm_w_in[j, :, mlstm_main:]), batch=batch,
                                    q_cols=MLSTM_HEADS * dqk, q_scale=dqk ** -0.5, gates_head_major=True)
            y = _mlstm(proj, gates_t, _gate_bias(mlstm_b_gates[j]), mlstm_norm_w[j].reshape(1, d),
                       batch=batch, d_model=d)
            xf = _outproj(y, mlstm_w_out_b, j, xf, g1, batch=batch)
        else:
            heads = d // FOX_HEAD_DIM
            wg = jnp.pad(fox_w_in[j, :, fox_main:], ((0, 0), (0, LANES - heads))).astype(BF16)
            proj, gates = _inproj(xf, nw, sc1, sh1, fox_w_in_b, j, fox_main, wg, batch=batch,
                                  q_cols=d, q_scale=FOX_HEAD_DIM ** -0.5 * LOG2E, gates_head_major=False)
            bias = jnp.pad(fox_b_f[j].astype(F32), (0, LANES - heads)).reshape(1, LANES)
            ck_aug = _cumgate(gates, bias, batch=batch, heads=heads, group=FOX_HEADS_PER_STEP)
            y = _fox_attention(proj, ck_aug, batch=batch, d_model=d)
            xf = _outproj(y, fox_w_out_b, j, xf, g1, batch=batch)
        xf = _ffn(xf, norm_ffn_w[layer].reshape(1, d), sc2, sh2, g2, ffn_w_gate_up_b, ffn_w_down_b, layer,
                  final_w, batch=batch, final_norm=(layer == depth - 1))
    return xf.reshape(batch, t, d)
```

```python
import functools

import jax
import jax.numpy as jnp
from jax import lax
from jax.experimental import pallas as pl
from jax.experimental.pallas import tpu as pltpu

F32 = jnp.float32
BF16 = jnp.bfloat16

EPS = 1e-6
GATE_SOFTCAP = 15.0
MLSTM_HEADS = 4
FOX_HEAD_DIM = 128
GATE_ROWS = 16
LANES = 128
NEG_BIG = -1e30
LOG2E = 1.4426950408889634

VMEM_LIMIT = 56 * 1024 * 1024


def _params(*sem):
    return pltpu.CompilerParams(dimension_semantics=sem, vmem_limit_bytes=VMEM_LIMIT)


def _tile(n, pref):
    t = min(n, pref)
    assert n % t == 0, (n, pref)
    return t


def _log_sigmoid(x):
    return jnp.minimum(x, 0.0) - jnp.log1p(jnp.exp(-jnp.abs(x)))


def _sigmoid(x):
    return 1.0 / (1.0 + jnp.exp(-x))


def _dot(a, b):
    return jnp.dot(a, b, preferred_element_type=F32)


def _dot_nt(a, b):
    return lax.dot_general(a, b, (((1,), (1,)), ((), ())), preferred_element_type=F32)


def _dot_tn(a, b):
    return lax.dot_general(a, b, (((0,), (0,)), ((), ())), preferred_element_type=F32)


def _modulated_norm(x, nw, sc, sh):
    y = x * lax.rsqrt(jnp.mean(x * x, axis=-1, keepdims=True) + EPS)
    return (y * nw) * (1.0 + sc) + sh


def _mod_kernel(c_ref, w_ref, b_ref, o_ref):
    c = c_ref[...]
    cs = (c * _sigmoid(c)).astype(BF16)
    o_ref[0] = _dot(cs, w_ref[0].astype(BF16)) + b_ref[0]


def _modulation(c_pad, ada_w, ada_b):
    depth, d, n6 = ada_w.shape
    rows = c_pad.shape[0]
    tn = _tile(n6, 1024)
    return pl.pallas_call(
        _mod_kernel,
        out_shape=jax.ShapeDtypeStruct((depth, rows, n6), F32),
        grid=(depth, n6 // tn),
        in_specs=[pl.BlockSpec((rows, d), lambda l, j: (0, 0)),
                  pl.BlockSpec((1, d, tn), lambda l, j: (l, 0, j)),
                  pl.BlockSpec((1, 1, tn), lambda l, j: (l, 0, j))],
        out_specs=pl.BlockSpec((1, rows, tn), lambda l, j: (l, 0, j)),
        compiler_params=_params("parallel", "parallel"),
        name="adaln_modulation",
    )(c_pad, ada_w, ada_b.reshape(depth, 1, n6))


def _inproj_kernel(x_ref, nw_ref, sc_ref, sh_ref, w_ref, wg_ref, o_ref, g_ref, h_ref, *, q_cols, q_scale,
                   gates_head_major):
    j = pl.program_id(1)
    tn = o_ref.shape[1]

    def project(hb):
        col = j * tn + lax.broadcasted_iota(jnp.int32, (1, tn), 1)
        scale = jnp.where(col < q_cols, q_scale, 1.0).astype(F32)
        o_ref[...] = (_dot(hb, w_ref[0]) * scale).astype(o_ref.dtype)

    @pl.when(j == 0)
    def _():
        hb = _modulated_norm(x_ref[...], nw_ref[...], sc_ref[0], sh_ref[0]).astype(BF16)
        h_ref[...] = hb
        project(hb)
        if gates_head_major:
            g_ref[0] = _dot_nt(wg_ref[...], hb)
        else:
            g_ref[...] = _dot(hb, wg_ref[...])

    @pl.when(j > 0)
    def _():
        project(h_ref[...])


def _inproj(x, nw, sc, sh, w, layer, nout, wg, *, batch, q_cols, q_scale, gates_head_major):
    n, d = x.shape
    t = n // batch
    tm = _tile(t, 1024)
    tn = _tile(nout, 2048)
    tpb = t // tm
    kern = functools.partial(_inproj_kernel, q_cols=q_cols, q_scale=q_scale, gates_head_major=gates_head_major)
    if gates_head_major:
        g_shape = jax.ShapeDtypeStruct((batch, GATE_ROWS, t), F32)
        g_spec = pl.BlockSpec((1, GATE_ROWS, tm), lambda i, j: (i // tpb, 0, i % tpb))
    else:
        g_shape = jax.ShapeDtypeStruct((n, LANES), F32)
        g_spec = pl.BlockSpec((tm, LANES), lambda i, j: (i, 0))
    return pl.pallas_call(
        kern,
        out_shape=(jax.ShapeDtypeStruct((n, nout), BF16), g_shape),
        grid=(n // tm, nout // tn),
        in_specs=[pl.BlockSpec((tm, d), lambda i, j: (i, 0)),
                  pl.BlockSpec((1, d), lambda i, j: (0, 0)),
                  pl.BlockSpec((1, 1, d), lambda i, j: (i // tpb, 0, 0)),
                  pl.BlockSpec((1, 1, d), lambda i, j: (i // tpb, 0, 0)),
                  pl.BlockSpec((1, d, tn), lambda i, j: (layer, 0, j)),
                  pl.BlockSpec(wg.shape, lambda i, j: (0, 0))],
        out_specs=(pl.BlockSpec((tm, tn), lambda i, j: (i, j)), g_spec),
        scratch_shapes=[pltpu.VMEM((tm, d), BF16)],
        compiler_params=_params("parallel", "arbitrary"),
        name="norm_inproj",
    )(x, nw, sc, sh, w, wg)


def _outproj_kernel(y_ref, w_ref, x_ref, g_ref, o_ref):
    o_ref[...] = x_ref[...] + g_ref[0] * _dot(y_ref[...], w_ref[0])


def _outproj(y, w, layer, x, gate, *, batch):
    n, kdim = y.shape
    d = w.shape[2]
    t = n // batch
    tm = _tile(t, 512)
    tn = d
    tpb = t // tm
    return pl.pallas_call(
        _outproj_kernel,
        out_shape=jax.ShapeDtypeStruct((n, d), F32),
        grid=(n // tm, d // tn),
        in_specs=[pl.BlockSpec((tm, kdim), lambda i, j: (i, 0)),
                  pl.BlockSpec((1, kdim, tn), lambda i, j: (layer, 0, j)),
                  pl.BlockSpec((tm, tn), lambda i, j: (i, j)),
                  pl.BlockSpec((1, 1, tn), lambda i, j: (i // tpb, 0, j))],
        out_specs=pl.BlockSpec((tm, tn), lambda i, j: (i, j)),
        compiler_params=_params("parallel", "arbitrary"),
        name="outproj_residual",
    )(y, w, x, gate)


def _ffn_kernel(x_ref, nw_ref, sc_ref, sh_ref, g2_ref, wg_ref, wu_ref, wd_ref, fw_ref, o_ref, h_ref, *, final_norm):
    f = pl.program_id(1)

    def contribution(h):
        g = _dot(h, wg_ref[0])
        u = _dot(h, wu_ref[0])
        a = (g * _sigmoid(g) * u).astype(BF16)
        return g2_ref[0] * _dot(a, wd_ref[0])

    @pl.when(f == 0)
    def _():
        x = x_ref[...]
        hb = _modulated_norm(x, nw_ref[...], sc_ref[0], sh_ref[0]).astype(BF16)
        h_ref[...] = hb
        o_ref[...] = x + contribution(hb)

    @pl.when(f > 0)
    def _():
        o_ref[...] += contribution(h_ref[...])

    if final_norm:
        @pl.when(f == pl.num_programs(1) - 1)
        def _():
            out = o_ref[...]
            o_ref[...] = out * lax.rsqrt(jnp.mean(out * out, axis=-1, keepdims=True) + EPS) * fw_ref[...]


def _ffn(x, nw, sc, sh, g2, w_gate_up, w_down, layer, final_w, *, batch, final_norm):
    n, d = x.shape
    dff = w_down.shape[1]
    t = n // batch
    tm = _tile(t, 1024)
    tf = _tile(dff, 512)
    tpb = t // tm
    nf = dff // tf
    vec = lambda: pl.BlockSpec((1, 1, d), lambda i, f: (i // tpb, 0, 0))
    return pl.pallas_call(
        functools.partial(_ffn_kernel, final_norm=final_norm),
        out_shape=jax.ShapeDtypeStruct((n, d), F32),
        grid=(n // tm, nf),
        in_specs=[pl.BlockSpec((tm, d), lambda i, f: (i, 0)),
                  pl.BlockSpec((1, d), lambda i, f: (0, 0)),
                  vec(), vec(), vec(),
                  pl.BlockSpec((1, d, tf), lambda i, f: (layer, 0, f)),
                  pl.BlockSpec((1, d, tf), lambda i, f: (layer, 0, nf + f)),
                  pl.BlockSpec((1, tf, d), lambda i, f: (layer, f, 0)),
                  pl.BlockSpec((1, d), lambda i, f: (0, 0))],
        out_specs=pl.BlockSpec((tm, d), lambda i, f: (i, 0)),
        scratch_shapes=[pltpu.VMEM((tm, d), BF16)],
        compiler_params=_params("parallel", "arbitrary"),
        name="swiglu_ffn",
    )(x, nw, sc, sh, g2, w_gate_up, w_gate_up, w_down, final_w)


def _mlstm_kernel(q_ref, k_ref, v_ref, o_ref, g_ref, bias_ref, nw_ref, y_ref, c_ref, n_ref, m_ref, *, heads):
    ci = pl.program_id(1)
    L = q_ref.shape[0]
    dqk = q_ref.shape[1] // heads
    dv = v_ref.shape[1] // heads

    @pl.when(ci == 0)
    def _():
        c_ref[...] = jnp.zeros_like(c_ref)
        n_ref[...] = jnp.zeros_like(n_ref)
        m_ref[...] = jnp.zeros_like(m_ref)

    ones = jnp.ones((L, LANES), BF16)
    ones_dv = jnp.ones((dv, LANES), BF16)
    q = [q_ref[:, h * dqk:(h + 1) * dqk] for h in range(heads)]
    k = [k_ref[:, h * dqk:(h + 1) * dqk] for h in range(heads)]
    qk = [_dot_nt(q[h], k[h]) for h in range(heads)]
    qc = [_dot(q[h], c_ref[h].astype(BF16)) for h in range(heads)]
    qn = [_dot(q[h], n_ref[h].astype(BF16)) for h in range(heads)]

    gates = g_ref[0] + bias_ref[...]
    gates = GATE_SOFTCAP * jnp.tanh(gates / GATE_SOFTCAP)
    row = lax.broadcasted_iota(jnp.int32, (L, L), 0)
    col = lax.broadcasted_iota(jnp.int32, (L, L), 1)
    causal = row >= col
    eye = row == col

    for h in range(heads):
        li_row = gates[h:h + 1, :]
        lf_row = _log_sigmoid(gates[heads + h:heads + h + 1, :])
        b_col = jnp.sum(jnp.where(causal, lf_row, 0.0), axis=1, keepdims=True)
        b_row = jnp.sum(jnp.where(eye, b_col, 0.0), axis=0, keepdims=True)
        li_col = jnp.sum(jnp.where(eye, li_row, 0.0), axis=1, keepdims=True)
        g_tot = jnp.sum(lf_row, axis=1, keepdims=True)

        m_prev = m_ref[h:h + 1, 0:1]
        dmat = jnp.where(causal, b_col + (li_row - b_row), -jnp.inf)
        inter_log = b_col + m_prev
        m_t = jnp.maximum(inter_log, jnp.max(dmat, axis=1, keepdims=True))
        w_intra = jnp.exp(dmat - m_t)
        w_inter = jnp.exp(inter_log - m_t)

        v = v_ref[:, h * dv:(h + 1) * dv]
        s = (qk[h] * w_intra).astype(BF16)
        num = w_inter * qc[h] + _dot(s, v)
        den = w_inter * qn[h][:, 0:1] + _dot(s, ones)[:, 0:1]
        r = 1.0 / jnp.maximum(jnp.abs(den), jnp.exp(-m_t))
        sumsq = _dot((num * num).astype(BF16), ones_dv)[:, 0:1]
        rms_h = r * jnp.sqrt(sumsq * (1.0 / dv))
        row_scale = r * lax.rsqrt(rms_h * rms_h + EPS)

        tail = g_tot - b_col + li_col
        m_new = jnp.maximum(g_tot + m_prev, jnp.max(tail, axis=0, keepdims=True))
        w_tail = jnp.exp(tail - m_new)
        decay = jnp.exp(g_tot + m_prev - m_new)
        kw = (k[h].astype(F32) * w_tail).astype(BF16)
        c_ref[h] = decay * c_ref[h] + _dot_tn(kw, v)
        n_ref[h] = decay * n_ref[h] + _dot_tn(kw, ones)
        m_ref[h:h + 1, :] = jnp.broadcast_to(m_new, (1, LANES))

        half_hs = (num * (0.5 * row_scale) * nw_ref[:, h * dv:(h + 1) * dv]).astype(BF16)
        t = jnp.tanh(o_ref[:, h * dv:(h + 1) * dv] * 0.5)
        y_ref[:, h * dv:(h + 1) * dv] = (half_hs + half_hs * t).astype(y_ref.dtype)


def _mlstm(proj, gates_t, bias, norm_w, *, batch, d_model):
    n = proj.shape[0]
    t = n // batch
    heads = MLSTM_HEADS
    dqk = d_model // (2 * heads)
    dv = d_model // heads
    L = _tile(t, 256)
    nc = t // L
    qk_w = heads * dqk
    assert (2 * qk_w) % (heads * dv) == 0 and heads <= 8
    v_blk = (2 * qk_w) // (heads * dv)
    return pl.pallas_call(
        functools.partial(_mlstm_kernel, heads=heads),
        out_shape=jax.ShapeDtypeStruct((n, heads * dv), BF16),
        grid=(batch, nc),
        in_specs=[pl.BlockSpec((L, qk_w), lambda b, c: (b * nc + c, 0)),
                  pl.BlockSpec((L, qk_w), lambda b, c: (b * nc + c, 1)),
                  pl.BlockSpec((L, heads * dv), lambda b, c: (b * nc + c, v_blk)),
                  pl.BlockSpec((L, heads * dv), lambda b, c: (b * nc + c, v_blk + 1)),
                  pl.BlockSpec((1, GATE_ROWS, L), lambda b, c: (b, 0, c)),
                  pl.BlockSpec((GATE_ROWS, 1), lambda b, c: (0, 0)),
                  pl.BlockSpec((1, heads * dv), lambda b, c: (0, 0))],
        out_specs=pl.BlockSpec((L, heads * dv), lambda b, c: (b * nc + c, 0)),
        scratch_shapes=[pltpu.VMEM((heads, dqk, dv), F32), pltpu.VMEM((heads, dqk, LANES), F32),
                        pltpu.VMEM((8, LANES), F32)],
        compiler_params=_params("parallel", "arbitrary"),
        name="mlstm_chunkwise",
    )(proj, proj, proj, proj, gates_t, bias, norm_w)


BIAS_PIECES = 3


def _split_bf16(x):
    p0 = x.astype(BF16)
    r1 = x - p0.astype(F32)
    p1 = r1.astype(BF16)
    p2 = (r1 - p1.astype(F32)).astype(BF16)
    return p0, p1, p2


def _cumgate_kernel(g_ref, bias_ref, o_ref, carry_ref, *, group):
    @pl.when(pl.program_id(1) == 0)
    def _():
        carry_ref[...] = jnp.zeros_like(carry_ref)

    lf = _log_sigmoid(g_ref[...] + bias_ref[...])
    tc = lf.shape[0]
    row = lax.broadcasted_iota(jnp.int32, (tc, tc), 0)
    col = lax.broadcasted_iota(jnp.int32, (tc, tc), 1)
    lower = jnp.where(row >= col, 1.0, 0.0).astype(BF16)
    cum = carry_ref[0:1, :]
    for piece in _split_bf16(lf):
        cum = cum + _dot(lower, piece)
    carry_ref[...] = jnp.broadcast_to(cum[tc - 1:tc, :], carry_ref.shape)

    width = o_ref.shape[1]
    src = lax.broadcasted_iota(jnp.int32, (LANES, width), 0)
    dst = lax.broadcasted_iota(jnp.int32, (LANES, width), 1)
    dst_head = (dst // LANES) * group + dst % group
    dst_piece = (dst % LANES) // group
    out = jnp.zeros((tc, width), F32)
    for j, piece in enumerate(_split_bf16(cum * LOG2E)):
        sel = jnp.where((dst_head == src) & (dst_piece == j), 1.0, 0.0).astype(BF16)
        out = out + _dot(piece, sel)
    o_ref[...] = out.astype(o_ref.dtype)


def _cumgate(gates, bias, *, batch, heads, group):
    n, lanes = gates.shape
    t = n // batch
    tc = _tile(t, 512)
    nc = t // tc
    assert heads % group == 0 and group & (group - 1) == 0 and BIAS_PIECES * group <= LANES
    width = (heads // group) * LANES
    return pl.pallas_call(
        functools.partial(_cumgate_kernel, group=group),
        out_shape=jax.ShapeDtypeStruct((n, width), BF16),
        grid=(batch, nc),
        in_specs=[pl.BlockSpec((tc, lanes), lambda b, c: (b * nc + c, 0)),
                  pl.BlockSpec((1, lanes), lambda b, c: (0, 0))],
        out_specs=pl.BlockSpec((tc, width), lambda b, c: (b * nc + c, 0)),
        scratch_shapes=[pltpu.VMEM((8, lanes), F32)],
        compiler_params=_params("parallel", "arbitrary"),
        name="fox_cum_log_forget",
    )(gates, bias)


def _fox_kernel(q_ref, k_ref, ck_ref, v_ref, o_ref, s_ref, mb_ref, m_ref, l_ref, acc_ref, *, tk):
    qi = pl.program_id(2)
    tq = q_ref.shape[0]
    dh = FOX_HEAD_DIM
    nh = q_ref.shape[1] // dh
    assert tq == tk
    lane = lax.broadcasted_iota(jnp.int32, (tq, LANES), 1)
    q_aug = []
    for hd in range(nh):
        minus_one = jnp.where((lane % nh == hd) & (lane < BIAS_PIECES * nh), -1.0, 0.0).astype(BF16)
        q_aug.append(jnp.concatenate([q_ref[:, hd * dh:(hd + 1) * dh], minus_one], axis=1))

    def head_scores(hd, ki, slot):
        start = pl.multiple_of(ki * tk, tk)
        k_aug = jnp.concatenate([k_ref[pl.ds(start, tk), hd * dh:(hd + 1) * dh],
                                 ck_ref[pl.ds(start, tk), :]], axis=1)
        s = _dot_nt(k_aug, q_aug[hd])
        s_ref[slot, hd] = s
        mb_ref[slot, hd] = jnp.max(s, axis=0, keepdims=True)

    def head_softmax_pv(hd, ki, slot, keep, first):
        start = pl.multiple_of(ki * tk, tk)
        v = v_ref[pl.ds(start, tk), hd * dh:(hd + 1) * dh]
        for c0 in range(0, tq, QCHUNK):
            qs = slice(c0, c0 + QCHUNK)
            s = s_ref[slot, hd, :, qs]
            if keep is not None:
                s = jnp.where(keep[:, qs], s, -jnp.inf)
                m_blk = jnp.max(s, axis=0, keepdims=True)
            else:
                m_blk = mb_ref[slot, hd, :, qs]
            m_new = m_blk if first else jnp.maximum(m_ref[hd, :, qs], m_blk)
            p = jnp.exp2(s - m_new)
            l_blk = jnp.sum(p, axis=0, keepdims=True)
            pv = _dot_tn(v, p.astype(BF16))
            if first:
                l_ref[hd, :, qs] = l_blk
                acc_ref[hd, :, qs] = pv
            else:
                alpha = jnp.exp2(m_ref[hd, :, qs] - m_new)
                l_ref[hd, :, qs] = alpha * l_ref[hd, :, qs] + l_blk
                acc_ref[hd, :, qs] = alpha * acc_ref[hd, :, qs] + pv
            m_ref[hd, :, qs] = m_new

    def scores(ki, slot):
        for hd in range(nh):
            head_scores(hd, ki, slot)

    def softmax_pv(ki, slot, keep=None, next_block=None, first=False):
        for hd in range(nh):
            if next_block is not None:
                head_scores(hd, *next_block)
            head_softmax_pv(hd, ki, slot, keep, first)

    key = lax.broadcasted_iota(jnp.int32, (tk, tq), 0)
    qry = lax.broadcasted_iota(jnp.int32, (tk, tq), 1)
    scores(qi, 0)
    softmax_pv(qi, 0, keep=key <= qry, next_block=(0, 1), first=True)

    def pair(j, carry):
        softmax_pv(2 * j, 1, next_block=(2 * j + 1, 0))
        softmax_pv(2 * j + 1, 0, next_block=(2 * j + 2, 1))
        return carry

    lax.fori_loop(0, (qi - 1) // 2, pair, 0)

    @pl.when(qi % 2 == 1)
    def _():
        softmax_pv(qi - 1, 1)

    @pl.when((qi % 2 == 0) & (qi > 0))
    def _():
        softmax_pv(qi - 2, 1, next_block=(qi - 1, 0))
        softmax_pv(qi - 1, 0)

    for hd in range(nh):
        o_ref[:, hd * dh:(hd + 1) * dh] = (acc_ref[hd] * (1.0 / l_ref[hd])).T.astype(o_ref.dtype)


FOX_HEADS_PER_STEP = 4
QCHUNK = 256


def _fox_attention(proj, ck_aug, *, batch, d_model):
    n = proj.shape[0]
    t = n // batch
    dh = FOX_HEAD_DIM
    nh = FOX_HEADS_PER_STEP
    groups = d_model // (nh * dh)
    assert ck_aug.shape[1] == groups * LANES
    tq = _tile(t, 512)
    nq = t // tq
    return pl.pallas_call(
        functools.partial(_fox_kernel, tk=tq),
        out_shape=jax.ShapeDtypeStruct((n, d_model), BF16),
        grid=(batch, groups, nq),
        in_specs=[pl.BlockSpec((tq, nh * dh), lambda b, g, i: (b * nq + i, g)),
                  pl.BlockSpec((t, nh * dh), lambda b, g, i: (b, groups + g)),
                  pl.BlockSpec((t, LANES), lambda b, g, i: (b, g)),
                  pl.BlockSpec((t, nh * dh), lambda b, g, i: (b, 2 * groups + g))],
        out_specs=pl.BlockSpec((tq, nh * dh), lambda b, g, i: (b * nq + i, g)),
        scratch_shapes=[pltpu.VMEM((2, nh, tq, tq), F32),
                        pltpu.VMEM((2, nh, 1, tq), F32),
                        pltpu.VMEM((nh, 1, tq), F32),
                        pltpu.VMEM((nh, 1, tq), F32),
                        pltpu.VMEM((nh, dh, tq), F32)],
        compiler_params=_params("parallel", "parallel", "arbitrary"),
        name="fox_attention",
    )(proj, proj, ck_aug, proj)


def _gate_weights_t(w_cols):
    g = w_cols.shape[1]
    return jnp.pad(w_cols.T, ((0, GATE_ROWS - g), (0, 0))).astype(BF16)


def _gate_bias(b):
    return jnp.pad(b.astype(F32), (0, GATE_ROWS - b.shape[0])).reshape(GATE_ROWS, 1)


def kernel(x, c, ada_w, ada_b, norm_mix_w, norm_ffn_w, mlstm_w_in, mlstm_b_gates, mlstm_norm_w, mlstm_w_out,
           fox_w_in, fox_b_f, fox_w_out, ffn_w_gate_up, ffn_w_down, final_norm_w):
    batch, t, d = x.shape
    depth = ada_w.shape[0]
    n = batch * t
    dqk = d // (2 * MLSTM_HEADS)
    mlstm_main = 2 * MLSTM_HEADS * dqk + 2 * d
    fox_main = 3 * d

    c_pad = jnp.pad(c, ((0, GATE_ROWS - batch), (0, 0)))
    mod = _modulation(c_pad, ada_w, ada_b)[:, :batch]
    mod = mod.reshape(depth, batch, 6, 1, d)

    mlstm_w_in_b, mlstm_w_out_b = mlstm_w_in.astype(BF16), mlstm_w_out.astype(BF16)
    fox_w_in_b, fox_w_out_b = fox_w_in.astype(BF16), fox_w_out.astype(BF16)
    ffn_w_gate_up_b, ffn_w_down_b = ffn_w_gate_up.astype(BF16), ffn_w_down.astype(BF16)

    xf = x.reshape(n, d)
    final_w = final_norm_w.reshape(1, d)
    for layer in range(depth):
        sh1, sc1, g1, sh2, sc2, g2 = [mod[layer, :, i] for i in range(6)]
        j = layer // 2
        nw = norm_mix_w[layer].reshape(1, d)
        if layer % 2 == 0:
            proj, gates_t = _inproj(xf, nw, sc1, sh1, mlstm_w_in_b, j, mlstm_main,
                                    _gate_weights_t(mlstm_w_in[j, :, mlstm_main:]), batch=batch,
                                    q_cols=MLSTM_HEADS * dqk, q_scale=dqk ** -0.5, gates_head_major=True)
            y = _mlstm(proj, gates_t, _gate_bias(mlstm_b_gates[j]), mlstm_norm_w[j].reshape(1, d),
                       batch=batch, d_model=d)
            xf = _outproj(y, mlstm_w_out_b, j, xf, g1, batch=batch)
        else:
            heads = d // FOX_HEAD_DIM
            wg = jnp.pad(fox_w_in[j, :, fox_main:], ((0, 0), (0, LANES - heads))).astype(BF16)
            proj, gates = _inproj(xf, nw, sc1, sh1, fox_w_in_b, j, fox_main, wg, batch=batch,
                                  q_cols=d, q_scale=FOX_HEAD_DIM ** -0.5 * LOG2E, gates_head_major=False)
            bias = jnp.pad(fox_b_f[j].astype(F32), (0, LANES - heads)).reshape(1, LANES)
            ck_aug = _cumgate(gates, bias, batch=batch, heads=heads, group=FOX_HEADS_PER_STEP)
            y = _fox_attention(proj, ck_aug, batch=batch, d_model=d)
            xf = _outproj(y, fox_w_out_b, j, xf, g1, batch=batch)
        xf = _ffn(xf, norm_ffn_w[layer].reshape(1, d), sc2, sh2, g2, ffn_w_gate_up_b, ffn_w_down_b, layer,
                  final_w, batch=batch, final_norm=(layer == depth - 1))
    return xf.reshape(batch, t, d)
```

```python
import functools

import jax
import jax.numpy as jnp
from jax import lax
from jax.experimental import pallas as pl
from jax.experimental.pallas import tpu as pltpu

F32 = jnp.float32
BF16 = jnp.bfloat16

EPS = 1e-6
GATE_SOFTCAP = 15.0
MLSTM_HEADS = 4
FOX_HEAD_DIM = 128
GATE_ROWS = 16
LANES = 128
NEG_BIG = -1e30
LOG2E = 1.4426950408889634

VMEM_LIMIT = 56 * 1024 * 1024


def _params(*sem):
    return pltpu.CompilerParams(dimension_semantics=sem, vmem_limit_bytes=VMEM_LIMIT)


def _tile(n, pref):
    t = min(n, pref)
    assert n % t == 0, (n, pref)
    return t


def _log_sigmoid(x):
    return jnp.minimum(x, 0.0) - jnp.log1p(jnp.exp(-jnp.abs(x)))


def _sigmoid(x):
    return 1.0 / (1.0 + jnp.exp(-x))


def _dot(a, b):
    return jnp.dot(a, b, preferred_element_type=F32)


def _dot_nt(a, b):
    return lax.dot_general(a, b, (((1,), (1,)), ((), ())), preferred_element_type=F32)


def _dot_tn(a, b):
    return lax.dot_general(a, b, (((0,), (0,)), ((), ())), preferred_element_type=F32)


def _modulated_norm(x, nw, sc, sh):
    y = x * lax.rsqrt(jnp.mean(x * x, axis=-1, keepdims=True) + EPS)
    return (y * nw) * (1.0 + sc) + sh


def _mod_kernel(c_ref, w_ref, b_ref, o_ref):
    c = c_ref[...]
    cs = (c * _sigmoid(c)).astype(BF16)
    o_ref[0] = _dot(cs, w_ref[0].astype(BF16)) + b_ref[0]


def _modulation(c_pad, ada_w, ada_b):
    depth, d, n6 = ada_w.shape
    rows = c_pad.shape[0]
    tn = _tile(n6, 1024)
    return pl.pallas_call(
        _mod_kernel,
        out_shape=jax.ShapeDtypeStruct((depth, rows, n6), F32),
        grid=(depth, n6 // tn),
        in_specs=[pl.BlockSpec((rows, d), lambda l, j: (0, 0)),
                  pl.BlockSpec((1, d, tn), lambda l, j: (l, 0, j)),
                  pl.BlockSpec((1, 1, tn), lambda l, j: (l, 0, j))],
        out_specs=pl.BlockSpec((1, rows, tn), lambda l, j: (l, 0, j)),
        compiler_params=_params("parallel", "parallel"),
        name="adaln_modulation",
    )(c_pad, ada_w, ada_b.reshape(depth, 1, n6))


def _inproj_kernel(x_ref, nw_ref, sc_ref, sh_ref, w_ref, wg_ref, o_ref, g_ref, h_ref, *, q_cols, q_scale,
                   gates_head_major):
    j = pl.program_id(1)
    tn = o_ref.shape[1]

    def project(hb):
        col = j * tn + lax.broadcasted_iota(jnp.int32, (1, tn), 1)
        scale = jnp.where(col < q_cols, q_scale, 1.0).astype(F32)
        o_ref[...] = (_dot(hb, w_ref[0]) * scale).astype(o_ref.dtype)

    @pl.when(j == 0)
    def _():
        hb = _modulated_norm(x_ref[...], nw_ref[...], sc_ref[0], sh_ref[0]).astype(BF16)
        h_ref[...] = hb
        project(hb)
        if gates_head_major:
            g_ref[0] = _dot_nt(wg_ref[...], hb)
        else:
            g_ref[...] = _dot(hb, wg_ref[...])

    @pl.when(j > 0)
    def _():
        project(h_ref[...])


def _inproj(x, nw, sc, sh, w, layer, nout, wg, *, batch, q_cols, q_scale, gates_head_major):
    n, d = x.shape
    t = n // batch
    tm = _tile(t, 1024)
    tn = _tile(nout, 2048)
    tpb = t // tm
    kern = functools.partial(_inproj_kernel, q_cols=q_cols, q_scale=q_scale, gates_head_major=gates_head_major)
    if gates_head_major:
        g_shape = jax.ShapeDtypeStruct((batch, GATE_ROWS, t), F32)
        g_spec = pl.BlockSpec((1, GATE_ROWS, tm), lambda i, j: (i // tpb, 0, i % tpb))
    else:
        g_shape = jax.ShapeDtypeStruct((n, LANES), F32)
        g_spec = pl.BlockSpec((tm, LANES), lambda i, j: (i, 0))
    return pl.pallas_call(
        kern,
        out_shape=(jax.ShapeDtypeStruct((n, nout), BF16), g_shape),
        grid=(n // tm, nout // tn),
        in_specs=[pl.BlockSpec((tm, d), lambda i, j: (i, 0)),
                  pl.BlockSpec((1, d), lambda i, j: (0, 0)),
                  pl.BlockSpec((1, 1, d), lambda i, j: (i // tpb, 0, 0)),
                  pl.BlockSpec((1, 1, d), lambda i, j: (i // tpb, 0, 0)),
                  pl.BlockSpec((1, d, tn), lambda i, j: (layer, 0, j)),
                  pl.BlockSpec(wg.shape, lambda i, j: (0, 0))],
        out_specs=(pl.BlockSpec((tm, tn), lambda i, j: (i, j)), g_spec),
        scratch_shapes=[pltpu.VMEM((tm, d), BF16)],
        compiler_params=_params("parallel", "arbitrary"),
        name="norm_inproj",
    )(x, nw, sc, sh, w, wg)


def _outproj_kernel(y_ref, w_ref, x_ref, g_ref, o_ref):
    o_ref[...] = x_ref[...] + g_ref[0] * _dot(y_ref[...], w_ref[0])


def _outproj(y, w, layer, x, gate, *, batch):
    n, kdim = y.shape
    d = w.shape[2]
    t = n // batch
    tm = _tile(t, 512)
    tn = d
    tpb = t // tm
    return pl.pallas_call(
        _outproj_kernel,
        out_shape=jax.ShapeDtypeStruct((n, d), F32),
        grid=(n // tm, d // tn),
        in_specs=[pl.BlockSpec((tm, kdim), lambda i, j: (i, 0)),
                  pl.BlockSpec((1, kdim, tn), lambda i, j: (layer, 0, j)),
                  pl.BlockSpec((tm, tn), lambda i, j: (i, j)),
                  pl.BlockSpec((1, 1, tn), lambda i, j: (i // tpb, 0, j))],
        out_specs=pl.BlockSpec((tm, tn), lambda i, j: (i, j)),
        compiler_params=_params("parallel", "arbitrary"),
        name="outproj_residual",
    )(y, w, x, gate)


def _ffn_kernel(x_ref, nw_ref, sc_ref, sh_ref, g2_ref, wg_ref, wu_ref, wd_ref, fw_ref, o_ref, h_ref, *, final_norm):
    f = pl.program_id(1)

    def contribution(h):
        g = _dot(h, wg_ref[0])
        u = _dot(h, wu_ref[0])
        a = (g * _sigmoid(g) * u).astype(BF16)
        return g2_ref[0] * _dot(a, wd_ref[0])

    @pl.when(f == 0)
    def _():
        x = x_ref[...]
        hb = _modulated_norm(x, nw_ref[...], sc_ref[0], sh_ref[0]).astype(BF16)
        h_ref[...] = hb
        o_ref[...] = x + contribution(hb)

    @pl.when(f > 0)
    def _():
        o_ref[...] += contribution(h_ref[...])

    if final_norm:
        @pl.when(f == pl.num_programs(1) - 1)
        def _():
            out = o_ref[...]
            o_ref[...] = out * lax.rsqrt(jnp.mean(out * out, axis=-1, keepdims=True) + EPS) * fw_ref[...]


def _ffn(x, nw, sc, sh, g2, w_gate_up, w_down, layer, final_w, *, batch, final_norm):
    n, d = x.shape
    dff = w_down.shape[1]
    t = n // batch
    tm = _tile(t, 1024)
    tf = _tile(dff, 512)
    tpb = t // tm
    nf = dff // tf
    vec = lambda: pl.BlockSpec((1, 1, d), lambda i, f: (i // tpb, 0, 0))
    return pl.pallas_call(
        functools.partial(_ffn_kernel, final_norm=final_norm),
        out_shape=jax.ShapeDtypeStruct((n, d), F32),
        grid=(n // tm, nf),
        in_specs=[pl.BlockSpec((tm, d), lambda i, f: (i, 0)),
                  pl.BlockSpec((1, d), lambda i, f: (0, 0)),
                  vec(), vec(), vec(),
                  pl.BlockSpec((1, d, tf), lambda i, f: (layer, 0, f)),
                  pl.BlockSpec((1, d, tf), lambda i, f: (layer, 0, nf + f)),
                  pl.BlockSpec((1, tf, d), lambda i, f: (layer, f, 0)),
                  pl.BlockSpec((1, d), lambda i, f: (0, 0))],
        out_specs=pl.BlockSpec((tm, d), lambda i, f: (i, 0)),
        scratch_shapes=[pltpu.VMEM((tm, d), BF16)],
        compiler_params=_params("parallel", "arbitrary"),
        name="swiglu_ffn",
    )(x, nw, sc, sh, g2, w_gate_up, w_gate_up, w_down, final_w)


def _mlstm_kernel(q_ref, k_ref, v_ref, o_ref, g_ref, bias_ref, nw_ref, y_ref, c_ref, n_ref, m_ref, *, heads):
    ci = pl.program_id(1)
    L = q_ref.shape[0]
    dqk = q_ref.shape[1] // heads
    dv = v_ref.shape[1] // heads

    @pl.when(ci == 0)
    def _():
        c_ref[...] = jnp.zeros_like(c_ref)
        n_ref[...] = jnp.zeros_like(n_ref)
        m_ref[...] = jnp.zeros_like(m_ref)

    ones = jnp.ones((L, LANES), BF16)
    ones_dv = jnp.ones((dv, LANES), BF16)
    q = [q_ref[:, h * dqk:(h + 1) * dqk] for h in range(heads)]
    k = [k_ref[:, h * dqk:(h + 1) * dqk] for h in range(heads)]
    qk = [_dot_nt(q[h], k[h]) for h in range(heads)]
    qc = [_dot(q[h], c_ref[h].astype(BF16)) for h in range(heads)]
    qn = [_dot(q[h], n_ref[h].astype(BF16)) for h in range(heads)]

    gates = g_ref[0] + bias_ref[...]
    gates = GATE_SOFTCAP * jnp.tanh(gates / GATE_SOFTCAP)
    row = lax.broadcasted_iota(jnp.int32, (L, L), 0)
    col = lax.broadcasted_iota(jnp.int32, (L, L), 1)
    causal = row >= col
    eye = row == col

    for h in range(heads):
        li_row = gates[h:h + 1, :]
        lf_row = _log_sigmoid(gates[heads + h:heads + h + 1, :])
        b_col = jnp.sum(jnp.where(causal, lf_row, 0.0), axis=1, keepdims=True)
        b_row = jnp.sum(jnp.where(eye, b_col, 0.0), axis=0, keepdims=True)
        li_col = jnp.sum(jnp.where(eye, li_row, 0.0), axis=1, keepdims=True)
        g_tot = jnp.sum(lf_row, axis=1, keepdims=True)

        m_prev = m_ref[h:h + 1, 0:1]
        dmat = jnp.where(causal, b_col + (li_row - b_row), -jnp.inf)
        inter_log = b_col + m_prev
        m_t = jnp.maximum(inter_log, jnp.max(dmat, axis=1, keepdims=True))
        w_intra = jnp.exp(dmat - m_t)
        w_inter = jnp.exp(inter_log - m_t)

        v = v_ref[:, h * dv:(h + 1) * dv]
        s = (qk[h] * w_intra).astype(BF16)
        num = w_inter * qc[h] + _dot(s, v)
        den = w_inter * qn[h][:, 0:1] + _dot(s, ones)[:, 0:1]
        r = 1.0 / jnp.maximum(jnp.abs(den), jnp.exp(-m_t))
        sumsq = _dot((num * num).astype(BF16), ones_dv)[:, 0:1]
        rms_h = r * jnp.sqrt(sumsq * (1.0 / dv))
        row_scale = r * lax.rsqrt(rms_h * rms_h + EPS)

        tail = g_tot - b_col + li_col
        m_new = jnp.maximum(g_tot + m_prev, jnp.max(tail, axis=0, keepdims=True))
        w_tail = jnp.exp(tail - m_new)
        decay = jnp.exp(g_tot + m_prev - m_new)
        kw = (k[h].astype(F32) * w_tail).astype(BF16)
        c_ref[h] = decay * c_ref[h] + _dot_tn(kw, v)
        n_ref[h] = decay * n_ref[h] + _dot_tn(kw, ones)
        m_ref[h:h + 1, :] = jnp.broadcast_to(m_new, (1, LANES))

        half_hs = (num * (0.5 * row_scale) * nw_ref[:, h * dv:(h + 1) * dv]).astype(BF16)
        t = jnp.tanh(o_ref[:, h * dv:(h + 1) * dv] * 0.5)
        y_ref[:, h * dv:(h + 1) * dv] = (half_hs + half_hs * t).astype(y_ref.dtype)


def _mlstm(proj, gates_t, bias, norm_w, *, batch, d_model):
    n = proj.shape[0]
    t = n // batch
    heads = MLSTM_HEADS
    dqk = d_model // (2 * heads)
    dv = d_model // heads
    L = _tile(t, 256)
    nc = t // L
    qk_w = heads * dqk
    assert (2 * qk_w) % (heads * dv) == 0 and heads <= 8
    v_blk = (2 * qk_w) // (heads * dv)
    return pl.pallas_call(
        functools.partial(_mlstm_kernel, heads=heads),
        out_shape=jax.ShapeDtypeStruct((n, heads * dv), BF16),
        grid=(batch, nc),
        in_specs=[pl.BlockSpec((L, qk_w), lambda b, c: (b * nc + c, 0)),
                  pl.BlockSpec((L, qk_w), lambda b, c: (b * nc + c, 1)),
                  pl.BlockSpec((L, heads * dv), lambda b, c: (b * nc + c, v_blk)),
                  pl.BlockSpec((L, heads * dv), lambda b, c: (b * nc + c, v_blk + 1)),
                  pl.BlockSpec((1, GATE_ROWS, L), lambda b, c: (b, 0, c)),
                  pl.BlockSpec((GATE_ROWS, 1), lambda b, c: (0, 0)),
                  pl.BlockSpec((1, heads * dv), lambda b, c: (0, 0))],
        out_specs=pl.BlockSpec((L, heads * dv), lambda b, c: (b * nc + c, 0)),
        scratch_shapes=[pltpu.VMEM((heads, dqk, dv), F32), pltpu.VMEM((heads, dqk, LANES), F32),
                        pltpu.VMEM((8, LANES), F32)],
        compiler_params=_params("parallel", "arbitrary"),
        name="mlstm_chunkwise",
    )(proj, proj, proj, proj, gates_t, bias, norm_w)


BIAS_PIECES = 3


def _split_bf16(x):
    p0 = x.astype(BF16)
    r1 = x - p0.astype(F32)
    p1 = r1.astype(BF16)
    p2 = (r1 - p1.astype(F32)).astype(BF16)
    return p0, p1, p2


def _cumgate_kernel(g_ref, bias_ref, o_ref, carry_ref, *, group):
    @pl.when(pl.program_id(1) == 0)
    def _():
        carry_ref[...] = jnp.zeros_like(carry_ref)

    lf = _log_sigmoid(g_ref[...] + bias_ref[...])
    tc = lf.shape[0]
    row = lax.broadcasted_iota(jnp.int32, (tc, tc), 0)
    col = lax.broadcasted_iota(jnp.int32, (tc, tc), 1)
    lower = jnp.where(row >= col, 1.0, 0.0).astype(BF16)
    cum = carry_ref[0:1, :]
    for piece in _split_bf16(lf):
        cum = cum + _dot(lower, piece)
    carry_ref[...] = jnp.broadcast_to(cum[tc - 1:tc, :], carry_ref.shape)

    width = o_ref.shape[1]
    src = lax.broadcasted_iota(jnp.int32, (LANES, width), 0)
    dst = lax.broadcasted_iota(jnp.int32, (LANES, width), 1)
    dst_head = (dst // LANES) * group + dst % group
    dst_piece = (dst % LANES) // group
    out = jnp.zeros((tc, width), F32)
    for j, piece in enumerate(_split_bf16(cum * LOG2E)):
        sel = jnp.where((dst_head == src) & (dst_piece == j), 1.0, 0.0).astype(BF16)
        out = out + _dot(piece, sel)
    o_ref[...] = out.astype(o_ref.dtype)


def _cumgate(gates, bias, *, batch, heads, group):
    n, lanes = gates.shape
    t = n // batch
    tc = _tile(t, 512)
    nc = t // tc
    assert heads % group == 0 and group & (group - 1) == 0 and BIAS_PIECES * group <= LANES
    width = (heads // group) * LANES
    return pl.pallas_call(
        functools.partial(_cumgate_kernel, group=group),
        out_shape=jax.ShapeDtypeStruct((n, width), BF16),
        grid=(batch, nc),
        in_specs=[pl.BlockSpec((tc, lanes), lambda b, c: (b * nc + c, 0)),
                  pl.BlockSpec((1, lanes), lambda b, c: (0, 0))],
        out_specs=pl.BlockSpec((tc, width), lambda b, c: (b * nc + c, 0)),
        scratch_shapes=[pltpu.VMEM((8, lanes), F32)],
        compiler_params=_params("parallel", "arbitrary"),
        name="fox_cum_log_forget",
    )(gates, bias)


def _fox_kernel(q_ref, k_ref, ck_ref, v_ref, o_ref, s_ref, mb_ref, m_ref, l_ref, acc_ref, *, tk):
    qi = pl.program_id(2)
    tq = q_ref.shape[0]
    dh = FOX_HEAD_DIM
    nh = q_ref.shape[1] // dh
    assert tq == tk
    lane = lax.broadcasted_iota(jnp.int32, (tq, LANES), 1)
    q_aug = []
    for hd in range(nh):
        minus_one = jnp.where((lane % nh == hd) & (lane < BIAS_PIECES * nh), -1.0, 0.0).astype(BF16)
        q_aug.append(jnp.concatenate([q_ref[:, hd * dh:(hd + 1) * dh], minus_one], axis=1))

    def head_scores(hd, ki, slot, diagonal=False):
        start = pl.multiple_of(ki * tk, tk)
        k_aug = jnp.concatenate([k_ref[pl.ds(start, tk), hd * dh:(hd + 1) * dh],
                                 ck_ref[pl.ds(start, tk), :]], axis=1)
        if diagonal:
            for c0 in range(0, tq, QCHUNK):
                s_ref[slot, hd, 0:c0 + QCHUNK, c0:c0 + QCHUNK] = _dot_nt(k_aug[0:c0 + QCHUNK],
                                                                           q_aug[hd][c0:c0 + QCHUNK])
        else:
            s = _dot_nt(k_aug, q_aug[hd])
            s_ref[slot, hd] = s
            mb_ref[slot, hd] = jnp.max(s, axis=0, keepdims=True)

    def head_softmax_pv(hd, ki, slot, diagonal, first):
        start = pl.multiple_of(ki * tk, tk)
        for c0 in range(0, tq, QCHUNK):
            qs = slice(c0, c0 + QCHUNK)
            nk = c0 + QCHUNK if diagonal else tk
            v = v_ref[pl.ds(start, nk), hd * dh:(hd + 1) * dh]
            s = s_ref[slot, hd, 0:nk, qs]
            if diagonal:
                key = lax.broadcasted_iota(jnp.int32, (nk, QCHUNK), 0)
                qry = lax.broadcasted_iota(jnp.int32, (nk, QCHUNK), 1) + c0
                s = jnp.where(key <= qry, s, -jnp.inf)
                m_blk = jnp.max(s, axis=0, keepdims=True)
            else:
                m_blk = mb_ref[slot, hd, :, qs]
            m_new = m_blk if first else jnp.maximum(m_ref[hd, :, qs], m_blk)
            p = jnp.exp2(s - m_new)
            l_blk = jnp.sum(p, axis=0, keepdims=True)
            pv = _dot_tn(v, p.astype(BF16))
            if first:
                l_ref[hd, :, qs] = l_blk
                acc_ref[hd, :, qs] = pv
            else:
                alpha = jnp.exp2(m_ref[hd, :, qs] - m_new)
                l_ref[hd, :, qs] = alpha * l_ref[hd, :, qs] + l_blk
                acc_ref[hd, :, qs] = alpha * acc_ref[hd, :, qs] + pv
            m_ref[hd, :, qs] = m_new

    def softmax_pv(ki, slot, diagonal=False, next_block=None, first=False):
        for hd in range(nh):
            if next_block is not None:
                head_scores(hd, *next_block)
            head_softmax_pv(hd, ki, slot, diagonal, first)

    for hd in range(nh):
        head_scores(hd, qi, 0, diagonal=True)
    softmax_pv(qi, 0, diagonal=True, next_block=(0, 1), first=True)

    def pair(j, carry):
        softmax_pv(2 * j, 1, next_block=(2 * j + 1, 0))
        softmax_pv(2 * j + 1, 0, next_block=(2 * j + 2, 1))
        return carry

    lax.fori_loop(0, (qi - 1) // 2, pair, 0)

    @pl.when(qi % 2 == 1)
    def _():
        softmax_pv(qi - 1, 1)

    @pl.when((qi % 2 == 0) & (qi > 0))
    def _():
        softmax_pv(qi - 2, 1, next_block=(qi - 1, 0))
        softmax_pv(qi - 1, 0)

    for hd in range(nh):
        o_ref[:, hd * dh:(hd + 1) * dh] = (acc_ref[hd] * (1.0 / l_ref[hd])).T.astype(o_ref.dtype)


FOX_HEADS_PER_STEP = 4
QCHUNK = 256


def _fox_attention(proj, ck_aug, *, batch, d_model):
    n = proj.shape[0]
    t = n // batch
    dh = FOX_HEAD_DIM
    nh = FOX_HEADS_PER_STEP
    groups = d_model // (nh * dh)
    assert ck_aug.shape[1] == groups * LANES
    tq = _tile(t, 512)
    nq = t // tq
    return pl.pallas_call(
        functools.partial(_fox_kernel, tk=tq),
        out_shape=jax.ShapeDtypeStruct((n, d_model), BF16),
        grid=(batch, groups, nq),
        in_specs=[pl.BlockSpec((tq, nh * dh), lambda b, g, i: (b * nq + i, g)),
                  pl.BlockSpec((t, nh * dh), lambda b, g, i: (b, groups + g)),
                  pl.BlockSpec((t, LANES), lambda b, g, i: (b, g)),
                  pl.BlockSpec((t, nh * dh), lambda b, g, i: (b, 2 * groups + g))],
        out_specs=pl.BlockSpec((tq, nh * dh), lambda b, g, i: (b * nq + i, g)),
        scratch_shapes=[pltpu.VMEM((2, nh, tq, tq), F32),
                        pltpu.VMEM((2, nh, 1, tq), F32),
                        pltpu.VMEM((nh, 1, tq), F32),
                        pltpu.VMEM((nh, 1, tq), F32),
                        pltpu.VMEM((nh, dh, tq), F32)],
        compiler_params=_params("parallel", "parallel", "arbitrary"),
        name="fox_attention",
    )(proj, proj, ck_aug, proj)


def _gate_weights_t(w_cols):
    g = w_cols.shape[1]
    return jnp.pad(w_cols.T, ((0, GATE_ROWS - g), (0, 0))).astype(BF16)


def _gate_bias(b):
    return jnp.pad(b.astype(F32), (0, GATE_ROWS - b.shape[0])).reshape(GATE_ROWS, 1)


def kernel(x, c, ada_w, ada_b, norm_mix_w, norm_ffn_w, mlstm_w_in, mlstm_b_gates, mlstm_norm_w, mlstm_w_out,
           fox_w_in, fox_b_f, fox_w_out, ffn_w_gate_up, ffn_w_down, final_norm_w):
    batch, t, d = x.shape
    depth = ada_w.shape[0]
    n = batch * t
    dqk = d // (2 * MLSTM_HEADS)
    mlstm_main = 2 * MLSTM_HEADS * dqk + 2 * d
    fox_main = 3 * d

    c_pad = jnp.pad(c, ((0, GATE_ROWS - batch), (0, 0)))
    mod = _modulation(c_pad, ada_w, ada_b)[:, :batch]
    mod = mod.reshape(depth, batch, 6, 1, d)

    mlstm_w_in_b, mlstm_w_out_b = mlstm_w_in.astype(BF16), mlstm_w_out.astype(BF16)
    fox_w_in_b, fox_w_out_b = fox_w_in.astype(BF16), fox_w_out.astype(BF16)
    ffn_w_gate_up_b, ffn_w_down_b = ffn_w_gate_up.astype(BF16), ffn_w_down.astype(BF16)

    xf = x.reshape(n, d)
    final_w = final_norm_w.reshape(1, d)
    for layer in range(depth):
        sh1, sc1, g1, sh2, sc2, g2 = [mod[layer, :, i] for i in range(6)]
        j = layer // 2
        nw = norm_mix_w[layer].reshape(1, d)
        if layer % 2 == 0:
            proj, gates_t = _inproj(xf, nw, sc1, sh1, mlstm_w_in_b, j, mlstm_main,
                                    _gate_weights_t(mlstm_w_in[j, :, mlstm_main:]), batch=batch,
                                    q_cols=MLSTM_HEADS * dqk, q_scale=dqk ** -0.5, gates_head_major=True)
            y = _mlstm(proj, gates_t, _gate_bias(mlstm_b_gates[j]), mlstm_norm_w[j].reshape(1, d),
                       batch=batch, d_model=d)
            xf = _outproj(y, mlstm_w_out_b, j, xf, g1, batch=batch)
        else:
            heads = d // FOX_HEAD_DIM
            wg = jnp.pad(fox_w_in[j, :, fox_main:], ((0, 0), (0, LANES - heads))).astype(BF16)
            proj, gates = _inproj(xf, nw, sc1, sh1, fox_w_in_b, j, fox_main, wg, batch=batch,
                                  q_cols=d, q_scale=FOX_HEAD_DIM ** -0.5 * LOG2E, gates_head_major=False)
            bias = jnp.pad(fox_b_f[j].astype(F32), (0, LANES - heads)).reshape(1, LANES)
            ck_aug = _cumgate(gates, bias, batch=batch, heads=heads, group=FOX_HEADS_PER_STEP)
            y = _fox_attention(proj, ck_aug, batch=batch, d_model=d)
            xf = _outproj(y, fox_w_out_b, j, xf, g1, batch=batch)
        xf = _ffn(xf, norm_ffn_w[layer].reshape(1, d), sc2, sh2, g2, ffn_w_gate_up_b, ffn_w_down_b, layer,
                  final_w, batch=batch, final_norm=(layer == depth - 1))
    return xf.reshape(batch, t, d)
```

```python
import functools

import jax
import jax.numpy as jnp
from jax import lax
from jax.experimental import pallas as pl
from jax.experimental.pallas import tpu as pltpu

F32 = jnp.float32
BF16 = jnp.bfloat16

EPS = 1e-6
GATE_SOFTCAP = 15.0
MLSTM_HEADS = 4
FOX_HEAD_DIM = 128
GATE_ROWS = 16
LANES = 128
LOG2E = 1.4426950408889634

VMEM_LIMIT = 56 * 1024 * 1024


def _params(*sem):
    return pltpu.CompilerParams(dimension_semantics=sem, vmem_limit_bytes=VMEM_LIMIT)


def _tile(n, pref):
    t = min(n, pref)
    assert n % t == 0, (n, pref)
    return t


def _log_sigmoid(x):
    return jnp.minimum(x, 0.0) - jnp.log1p(jnp.exp(-jnp.abs(x)))


def _sigmoid(x):
    return 1.0 / (1.0 + jnp.exp(-x))


def _dot(a, b):
    return jnp.dot(a, b, preferred_element_type=F32)


def _dot_nt(a, b):
    return lax.dot_general(a, b, (((1,), (1,)), ((), ())), preferred_element_type=F32)


def _dot_tn(a, b):
    return lax.dot_general(a, b, (((0,), (0,)), ((), ())), preferred_element_type=F32)


def _modulated_norm(x, nw, sc, sh):
    y = x * lax.rsqrt(jnp.mean(x * x, axis=-1, keepdims=True) + EPS)
    return (y * nw) * (1.0 + sc) + sh


def _mod_kernel(c_ref, w_ref, b_ref, o_ref):
    c = c_ref[...]
    cs = (c * _sigmoid(c)).astype(BF16)
    o_ref[0] = _dot(cs, w_ref[0].astype(BF16)) + b_ref[0]


def _modulation(c_pad, ada_w, ada_b):
    depth, d, n6 = ada_w.shape
    rows = c_pad.shape[0]
    tn = _tile(n6, 1024)
    return pl.pallas_call(
        _mod_kernel,
        out_shape=jax.ShapeDtypeStruct((depth, rows, n6), F32),
        grid=(depth, n6 // tn),
        in_specs=[pl.BlockSpec((rows, d), lambda l, j: (0, 0)),
                  pl.BlockSpec((1, d, tn), lambda l, j: (l, 0, j)),
                  pl.BlockSpec((1, 1, tn), lambda l, j: (l, 0, j))],
        out_specs=pl.BlockSpec((1, rows, tn), lambda l, j: (l, 0, j)),
        compiler_params=_params("parallel", "parallel"),
        name="adaln_modulation",
    )(c_pad, ada_w, ada_b.reshape(depth, 1, n6))


def _inproj_kernel(x_ref, nw_ref, sc_ref, sh_ref, w_ref, wg_ref, o_ref, g_ref, h_ref, *, q_cols, q_scale,
                   gates_head_major):
    j = pl.program_id(1)
    tn = o_ref.shape[1]

    def project(hb):
        col = j * tn + lax.broadcasted_iota(jnp.int32, (1, tn), 1)
        scale = jnp.where(col < q_cols, q_scale, 1.0).astype(F32)
        o_ref[...] = (_dot(hb, w_ref[0]) * scale).astype(o_ref.dtype)

    @pl.when(j == 0)
    def _():
        hb = _modulated_norm(x_ref[...], nw_ref[...], sc_ref[0], sh_ref[0]).astype(BF16)
        h_ref[...] = hb
        project(hb)
        if gates_head_major:
            g_ref[0] = _dot_nt(wg_ref[...], hb)
        else:
            g_ref[...] = _dot(hb, wg_ref[...])

    @pl.when(j > 0)
    def _():
        project(h_ref[...])


def _inproj(x, nw, sc, sh, w, layer, nout, wg, *, batch, q_cols, q_scale, gates_head_major):
    n, d = x.shape
    t = n // batch
    tm = _tile(t, 1024)
    tn = _tile(nout, 2048)
    tpb = t // tm
    kern = functools.partial(_inproj_kernel, q_cols=q_cols, q_scale=q_scale, gates_head_major=gates_head_major)
    if gates_head_major:
        g_shape = jax.ShapeDtypeStruct((batch, GATE_ROWS, t), F32)
        g_spec = pl.BlockSpec((1, GATE_ROWS, tm), lambda i, j: (i // tpb, 0, i % tpb))
    else:
        g_shape = jax.ShapeDtypeStruct((n, LANES), F32)
        g_spec = pl.BlockSpec((tm, LANES), lambda i, j: (i, 0))
    return pl.pallas_call(
        kern,
        out_shape=(jax.ShapeDtypeStruct((n, nout), BF16), g_shape),
        grid=(n // tm, nout // tn),
        in_specs=[pl.BlockSpec((tm, d), lambda i, j: (i, 0)),
                  pl.BlockSpec((1, d), lambda i, j: (0, 0)),
                  pl.BlockSpec((1, 1, d), lambda i, j: (i // tpb, 0, 0)),
                  pl.BlockSpec((1, 1, d), lambda i, j: (i // tpb, 0, 0)),
                  pl.BlockSpec((1, d, tn), lambda i, j: (layer, 0, j)),
                  pl.BlockSpec(wg.shape, lambda i, j: (0, 0))],
        out_specs=(pl.BlockSpec((tm, tn), lambda i, j: (i, j)), g_spec),
        scratch_shapes=[pltpu.VMEM((tm, d), BF16)],
        compiler_params=_params("parallel", "arbitrary"),
        name="norm_inproj",
    )(x, nw, sc, sh, w, wg)


def _outproj_kernel(y_ref, w_ref, x_ref, g_ref, o_ref):
    o_ref[...] = x_ref[...] + g_ref[0] * _dot(y_ref[...], w_ref[0])


def _outproj(y, w, layer, x, gate, *, batch):
    n, kdim = y.shape
    d = w.shape[2]
    t = n // batch
    tm = _tile(t, 512)
    tn = d
    tpb = t // tm
    return pl.pallas_call(
        _outproj_kernel,
        out_shape=jax.ShapeDtypeStruct((n, d), F32),
        grid=(n // tm, d // tn),
        in_specs=[pl.BlockSpec((tm, kdim), lambda i, j: (i, 0)),
                  pl.BlockSpec((1, kdim, tn), lambda i, j: (layer, 0, j)),
                  pl.BlockSpec((tm, tn), lambda i, j: (i, j)),
                  pl.BlockSpec((1, 1, tn), lambda i, j: (i // tpb, 0, j))],
        out_specs=pl.BlockSpec((tm, tn), lambda i, j: (i, j)),
        compiler_params=_params("parallel", "arbitrary"),
        name="outproj_residual",
    )(y, w, x, gate)


def _ffn_kernel(x_ref, nw_ref, sc_ref, sh_ref, g2_ref, wg_ref, wu_ref, wd_ref, fw_ref, o_ref, h_ref, *, final_norm):
    f = pl.program_id(1)

    def contribution(h):
        g = _dot(h, wg_ref[0])
        u = _dot(h, wu_ref[0])
        a = (g * _sigmoid(g) * u).astype(BF16)
        return g2_ref[0] * _dot(a, wd_ref[0])

    @pl.when(f == 0)
    def _():
        x = x_ref[...]
        hb = _modulated_norm(x, nw_ref[...], sc_ref[0], sh_ref[0]).astype(BF16)
        h_ref[...] = hb
        o_ref[...] = x + contribution(hb)

    @pl.when(f > 0)
    def _():
        o_ref[...] += contribution(h_ref[...])

    if final_norm:
        @pl.when(f == pl.num_programs(1) - 1)
        def _():
            out = o_ref[...]
            o_ref[...] = out * lax.rsqrt(jnp.mean(out * out, axis=-1, keepdims=True) + EPS) * fw_ref[...]


def _ffn(x, nw, sc, sh, g2, w_gate_up, w_down, layer, final_w, *, batch, final_norm):
    n, d = x.shape
    dff = w_down.shape[1]
    t = n // batch
    tm = _tile(t, 1024)
    tf = _tile(dff, 512)
    tpb = t // tm
    nf = dff // tf
    vec = lambda: pl.BlockSpec((1, 1, d), lambda i, f: (i // tpb, 0, 0))
    return pl.pallas_call(
        functools.partial(_ffn_kernel, final_norm=final_norm),
        out_shape=jax.ShapeDtypeStruct((n, d), F32),
        grid=(n // tm, nf),
        in_specs=[pl.BlockSpec((tm, d), lambda i, f: (i, 0)),
                  pl.BlockSpec((1, d), lambda i, f: (0, 0)),
                  vec(), vec(), vec(),
                  pl.BlockSpec((1, d, tf), lambda i, f: (layer, 0, f)),
                  pl.BlockSpec((1, d, tf), lambda i, f: (layer, 0, nf + f)),
                  pl.BlockSpec((1, tf, d), lambda i, f: (layer, f, 0)),
                  pl.BlockSpec((1, d), lambda i, f: (0, 0))],
        out_specs=pl.BlockSpec((tm, d), lambda i, f: (i, 0)),
        scratch_shapes=[pltpu.VMEM((tm, d), BF16)],
        compiler_params=_params("parallel", "arbitrary"),
        name="swiglu_ffn",
    )(x, nw, sc, sh, g2, w_gate_up, w_gate_up, w_down, final_w)


def _mlstm_kernel(q_ref, k_ref, v_ref, o_ref, g_ref, bias_ref, nw_ref, y_ref, c_ref, n_ref, m_ref, *, heads):
    ci = pl.program_id(1)
    L = q_ref.shape[0]
    dqk = q_ref.shape[1] // heads
    dv = v_ref.shape[1] // heads

    @pl.when(ci == 0)
    def _():
        c_ref[...] = jnp.zeros_like(c_ref)
        n_ref[...] = jnp.zeros_like(n_ref)
        m_ref[...] = jnp.zeros_like(m_ref)

    ones = jnp.ones((L, LANES), BF16)
    ones_dv = jnp.ones((dv, LANES), BF16)
    q = [q_ref[:, h * dqk:(h + 1) * dqk] for h in range(heads)]
    k = [k_ref[:, h * dqk:(h + 1) * dqk] for h in range(heads)]
    qk = [_dot_nt(q[h], k[h]) for h in range(heads)]
    qc = [_dot(q[h], c_ref[h].astype(BF16)) for h in range(heads)]
    qn = [_dot(q[h], n_ref[h].astype(BF16)) for h in range(heads)]

    gates = g_ref[0] + bias_ref[...]
    gates = GATE_SOFTCAP * jnp.tanh(gates / GATE_SOFTCAP)
    row = lax.broadcasted_iota(jnp.int32, (L, L), 0)
    col = lax.broadcasted_iota(jnp.int32, (L, L), 1)
    causal = row >= col
    eye = row == col

    for h in range(heads):
        li_row = gates[h:h + 1, :]
        lf_row = _log_sigmoid(gates[heads + h:heads + h + 1, :])
        b_col = jnp.sum(jnp.where(causal, lf_row, 0.0), axis=1, keepdims=True)
        b_row = jnp.sum(jnp.where(eye, b_col, 0.0), axis=0, keepdims=True)
        li_col = jnp.sum(jnp.where(eye, li_row, 0.0), axis=1, keepdims=True)
        g_tot = jnp.sum(lf_row, axis=1, keepdims=True)

        m_prev = m_ref[h:h + 1, 0:1]
        dmat = jnp.where(causal, b_col + (li_row - b_row), -jnp.inf)
        inter_log = b_col + m_prev
        m_t = jnp.maximum(inter_log, jnp.max(dmat, axis=1, keepdims=True))
        w_intra = jnp.exp(dmat - m_t)
        w_inter = jnp.exp(inter_log - m_t)

        v = v_ref[:, h * dv:(h + 1) * dv]
        s = (qk[h] * w_intra).astype(BF16)
        num = w_inter * qc[h] + _dot(s, v)
        den = w_inter * qn[h][:, 0:1] + _dot(s, ones)[:, 0:1]
        r = 1.0 / jnp.maximum(jnp.abs(den), jnp.exp(-m_t))
        sumsq = _dot((num * num).astype(BF16), ones_dv)[:, 0:1]
        rms_h = r * jnp.sqrt(sumsq * (1.0 / dv))
        row_scale = r * lax.rsqrt(rms_h * rms_h + EPS)

        tail = g_tot - b_col + li_col
        m_new = jnp.maximum(g_tot + m_prev, jnp.max(tail, axis=0, keepdims=True))
        w_tail = jnp.exp(tail - m_new)
        decay = jnp.exp(g_tot + m_prev - m_new)
        kw = (k[h].astype(F32) * w_tail).astype(BF16)
        c_ref[h] = decay * c_ref[h] + _dot_tn(kw, v)
        n_ref[h] = decay * n_ref[h] + _dot_tn(kw, ones)
        m_ref[h:h + 1, :] = jnp.broadcast_to(m_new, (1, LANES))

        half_hs = (num * (0.5 * row_scale) * nw_ref[:, h * dv:(h + 1) * dv]).astype(BF16)
        t = jnp.tanh(o_ref[:, h * dv:(h + 1) * dv] * 0.5)
        y_ref[:, h * dv:(h + 1) * dv] = (half_hs + half_hs * t).astype(y_ref.dtype)


def _mlstm(proj, gates_t, bias, norm_w, *, batch, d_model):
    n = proj.shape[0]
    t = n // batch
    heads = MLSTM_HEADS
    dqk = d_model // (2 * heads)
    dv = d_model // heads
    L = _tile(t, 256)
    nc = t // L
    qk_w = heads * dqk
    assert (2 * qk_w) % (heads * dv) == 0 and heads <= 8
    v_blk = (2 * qk_w) // (heads * dv)
    return pl.pallas_call(
        functools.partial(_mlstm_kernel, heads=heads),
        out_shape=jax.ShapeDtypeStruct((n, heads * dv), BF16),
        grid=(batch, nc),
        in_specs=[pl.BlockSpec((L, qk_w), lambda b, c: (b * nc + c, 0)),
                  pl.BlockSpec((L, qk_w), lambda b, c: (b * nc + c, 1)),
                  pl.BlockSpec((L, heads * dv), lambda b, c: (b * nc + c, v_blk)),
                  pl.BlockSpec((L, heads * dv), lambda b, c: (b * nc + c, v_blk + 1)),
                  pl.BlockSpec((1, GATE_ROWS, L), lambda b, c: (b, 0, c)),
                  pl.BlockSpec((GATE_ROWS, 1), lambda b, c: (0, 0)),
                  pl.BlockSpec((1, heads * dv), lambda b, c: (0, 0))],
        out_specs=pl.BlockSpec((L, heads * dv), lambda b, c: (b * nc + c, 0)),
        scratch_shapes=[pltpu.VMEM((heads, dqk, dv), F32), pltpu.VMEM((heads, dqk, LANES), F32),
                        pltpu.VMEM((8, LANES), F32)],
        compiler_params=_params("parallel", "arbitrary"),
        name="mlstm_chunkwise",
    )(proj, proj, proj, proj, gates_t, bias, norm_w)


BIAS_PIECES = 3


def _split_bf16(x):
    p0 = x.astype(BF16)
    r1 = x - p0.astype(F32)
    p1 = r1.astype(BF16)
    p2 = (r1 - p1.astype(F32)).astype(BF16)
    return p0, p1, p2


def _cumgate_kernel(g_ref, bias_ref, o_ref, carry_ref, *, group):
    @pl.when(pl.program_id(1) == 0)
    def _():
        carry_ref[...] = jnp.zeros_like(carry_ref)

    lf = _log_sigmoid(g_ref[...] + bias_ref[...])
    tc = lf.shape[0]
    row = lax.broadcasted_iota(jnp.int32, (tc, tc), 0)
    col = lax.broadcasted_iota(jnp.int32, (tc, tc), 1)
    lower = jnp.where(row >= col, 1.0, 0.0).astype(BF16)
    cum = carry_ref[0:1, :]
    for piece in _split_bf16(lf):
        cum = cum + _dot(lower, piece)
    carry_ref[...] = jnp.broadcast_to(cum[tc - 1:tc, :], carry_ref.shape)

    width = o_ref.shape[1]
    src = lax.broadcasted_iota(jnp.int32, (LANES, width), 0)
    dst = lax.broadcasted_iota(jnp.int32, (LANES, width), 1)
    dst_head = (dst // LANES) * group + dst % group
    dst_piece = (dst % LANES) // group
    out = jnp.zeros((tc, width), F32)
    for j, piece in enumerate(_split_bf16(cum * LOG2E)):
        sel = jnp.where((dst_head == src) & (dst_piece == j), 1.0, 0.0).astype(BF16)
        out = out + _dot(piece, sel)
    o_ref[...] = out.astype(o_ref.dtype)


def _cumgate(gates, bias, *, batch, heads, group):
    n, lanes = gates.shape
    t = n // batch
    tc = _tile(t, 512)
    nc = t // tc
    assert heads % group == 0 and group & (group - 1) == 0 and BIAS_PIECES * group <= LANES
    width = (heads // group) * LANES
    return pl.pallas_call(
        functools.partial(_cumgate_kernel, group=group),
        out_shape=jax.ShapeDtypeStruct((n, width), BF16),
        grid=(batch, nc),
        in_specs=[pl.BlockSpec((tc, lanes), lambda b, c: (b * nc + c, 0)),
                  pl.BlockSpec((1, lanes), lambda b, c: (0, 0))],
        out_specs=pl.BlockSpec((tc, width), lambda b, c: (b * nc + c, 0)),
        scratch_shapes=[pltpu.VMEM((8, lanes), F32)],
        compiler_params=_params("parallel", "arbitrary"),
        name="fox_cum_log_forget",
    )(gates, bias)


def _fox_kernel(q_ref, k_ref, ck_ref, v_ref, o_ref, *scratch, tk):
    per_step = q_ref.shape[0] // tk
    for sub in range(per_step):
        _fox_query_block(pl.program_id(2) * per_step + sub, slice(sub * tk, (sub + 1) * tk),
                         q_ref, k_ref, ck_ref, v_ref, o_ref, *scratch, tk=tk)


def _fox_query_block(qi, rows, q_ref, k_ref, ck_ref, v_ref, o_ref, s_ref, mb_ref, m_ref, l_ref, acc_ref, *, tk):
    tq = tk
    dh = FOX_HEAD_DIM
    nh = q_ref.shape[1] // dh
    lane = lax.broadcasted_iota(jnp.int32, (tq, LANES), 1)
    q_aug = []
    for hd in range(nh):
        minus_one = jnp.where((lane % nh == hd) & (lane < BIAS_PIECES * nh), -1.0, 0.0).astype(BF16)
        q_aug.append(jnp.concatenate([q_ref[rows, hd * dh:(hd + 1) * dh], minus_one], axis=1))

    def head_scores(hd, ki, slot, diagonal=False):
        start = pl.multiple_of(ki * tk, tk)
        k_aug = jnp.concatenate([k_ref[pl.ds(start, tk), hd * dh:(hd + 1) * dh],
                                 ck_ref[pl.ds(start, tk), :]], axis=1)
        if diagonal:
            for c0 in range(0, tq, QCHUNK):
                s_ref[slot, hd, 0:c0 + QCHUNK, c0:c0 + QCHUNK] = _dot_nt(k_aug[0:c0 + QCHUNK],
                                                                           q_aug[hd][c0:c0 + QCHUNK])
        else:
            s = _dot_nt(k_aug, q_aug[hd])
            s_ref[slot, hd] = s
            mb_ref[slot, hd] = jnp.max(s, axis=0, keepdims=True)

    def head_softmax_pv(hd, ki, slot, diagonal, first):
        start = pl.multiple_of(ki * tk, tk)
        for c0 in range(0, tq, QCHUNK):
            qs = slice(c0, c0 + QCHUNK)
            nk = c0 + QCHUNK if diagonal else tk
            v = v_ref[pl.ds(start, nk), hd * dh:(hd + 1) * dh]
            s = s_ref[slot, hd, 0:nk, qs]
            if diagonal:
                key = lax.broadcasted_iota(jnp.int32, (nk, QCHUNK), 0)
                qry = lax.broadcasted_iota(jnp.int32, (nk, QCHUNK), 1) + c0
                s = jnp.where(key <= qry, s, -jnp.inf)
                m_blk = jnp.max(s, axis=0, keepdims=True)
            else:
                m_blk = mb_ref[slot, hd, :, qs]
            m_new = m_blk if first else jnp.maximum(m_ref[hd, :, qs], m_blk)
            p = jnp.exp2(s - m_new)
            l_blk = jnp.sum(p, axis=0, keepdims=True)
            pv = _dot_tn(v, p.astype(BF16))
            if first:
                l_ref[hd, :, qs] = l_blk
                acc_ref[hd, :, qs] = pv
            else:
                alpha = jnp.exp2(m_ref[hd, :, qs] - m_new)
                l_ref[hd, :, qs] = alpha * l_ref[hd, :, qs] + l_blk
                acc_ref[hd, :, qs] = alpha * acc_ref[hd, :, qs] + pv
            m_ref[hd, :, qs] = m_new

    def softmax_pv(ki, slot, diagonal=False, next_block=None, first=False):
        for hd in range(nh):
            if next_block is not None:
                head_scores(hd, *next_block)
            head_softmax_pv(hd, ki, slot, diagonal, first)

    for hd in range(nh):
        head_scores(hd, qi, 0, diagonal=True)
    softmax_pv(qi, 0, diagonal=True, next_block=(0, 1), first=True)

    def pair(j, carry):
        softmax_pv(2 * j, 1, next_block=(2 * j + 1, 0))
        softmax_pv(2 * j + 1, 0, next_block=(2 * j + 2, 1))
        return carry

    lax.fori_loop(0, (qi - 1) // 2, pair, 0)

    @pl.when(qi % 2 == 1)
    def _():
        softmax_pv(qi - 1, 1)

    @pl.when((qi % 2 == 0) & (qi > 0))
    def _():
        softmax_pv(qi - 2, 1, next_block=(qi - 1, 0))
        softmax_pv(qi - 1, 0)

    for hd in range(nh):
        o_ref[rows, hd * dh:(hd + 1) * dh] = (acc_ref[hd] * (1.0 / l_ref[hd])).T.astype(o_ref.dtype)


FOX_HEADS_PER_STEP = 4
Q_BLOCKS_PER_STEP = 2
QCHUNK = 256


def _fox_attention(proj, ck_aug, *, batch, d_model):
    n = proj.shape[0]
    t = n // batch
    dh = FOX_HEAD_DIM
    nh = FOX_HEADS_PER_STEP
    groups = d_model // (nh * dh)
    assert ck_aug.shape[1] == groups * LANES
    tq = _tile(t, 512)
    rows = _tile(t, Q_BLOCKS_PER_STEP * tq)
    nq = t // rows
    return pl.pallas_call(
        functools.partial(_fox_kernel, tk=tq),
        out_shape=jax.ShapeDtypeStruct((n, d_model), BF16),
        grid=(batch, groups, nq),
        in_specs=[pl.BlockSpec((rows, nh * dh), lambda b, g, i: (b * nq + i, g)),
                  pl.BlockSpec((t, nh * dh), lambda b, g, i: (b, groups + g)),
                  pl.BlockSpec((t, LANES), lambda b, g, i: (b, g)),
                  pl.BlockSpec((t, nh * dh), lambda b, g, i: (b, 2 * groups + g))],
        out_specs=pl.BlockSpec((rows, nh * dh), lambda b, g, i: (b * nq + i, g)),
        scratch_shapes=[pltpu.VMEM((2, nh, tq, tq), F32),
                        pltpu.VMEM((2, nh, 1, tq), F32),
                        pltpu.VMEM((nh, 1, tq), F32),
                        pltpu.VMEM((nh, 1, tq), F32),
                        pltpu.VMEM((nh, dh, tq), F32)],
        compiler_params=_params("parallel", "parallel", "arbitrary"),
        name="fox_attention",
    )(proj, proj, ck_aug, proj)


def _gate_weights_t(w_cols):
    g = w_cols.shape[1]
    return jnp.pad(w_cols.T, ((0, GATE_ROWS - g), (0, 0))).astype(BF16)


def _gate_bias(b):
    return jnp.pad(b.astype(F32), (0, GATE_ROWS - b.shape[0])).reshape(GATE_ROWS, 1)


def kernel(x, c, ada_w, ada_b, norm_mix_w, norm_ffn_w, mlstm_w_in, mlstm_b_gates, mlstm_norm_w, mlstm_w_out,
           fox_w_in, fox_b_f, fox_w_out, ffn_w_gate_up, ffn_w_down, final_norm_w):
    batch, t, d = x.shape
    depth = ada_w.shape[0]
    n = batch * t
    dqk = d // (2 * MLSTM_HEADS)
    mlstm_main = 2 * MLSTM_HEADS * dqk + 2 * d
    fox_main = 3 * d

    c_pad = jnp.pad(c, ((0, GATE_ROWS - batch), (0, 0)))
    mod = _modulation(c_pad, ada_w, ada_b)[:, :batch]
    mod = mod.reshape(depth, batch, 6, 1, d)

    mlstm_w_in_b, mlstm_w_out_b = mlstm_w_in.astype(BF16), mlstm_w_out.astype(BF16)
    fox_w_in_b, fox_w_out_b = fox_w_in.astype(BF16), fox_w_out.astype(BF16)
    ffn_w_gate_up_b, ffn_w_down_b = ffn_w_gate_up.astype(BF16), ffn_w_down.astype(BF16)

    xf = x.reshape(n, d)
    final_w = final_norm_w.reshape(1, d)
    for layer in range(depth):
        sh1, sc1, g1, sh2, sc2, g2 = [mod[layer, :, i] for i in range(6)]
        j = layer // 2
        nw = norm_mix_w[layer].reshape(1, d)
        if layer % 2 == 0:
            proj, gates_t = _inproj(xf, nw, sc1, sh1, mlstm_w_in_b, j, mlstm_main,
                                    _gate_weights_t(mlstm_w_in[j, :, mlstm_main:]), batch=batch,
                                    q_cols=MLSTM_HEADS * dqk, q_scale=dqk ** -0.5, gates_head_major=True)
            y = _mlstm(proj, gates_t, _gate_bias(mlstm_b_gates[j]), mlstm_norm_w[j].reshape(1, d),
                       batch=batch, d_model=d)
            xf = _outproj(y, mlstm_w_out_b, j, xf, g1, batch=batch)
        else:
            heads = d // FOX_HEAD_DIM
            wg = jnp.pad(fox_w_in[j, :, fox_main:], ((0, 0), (0, LANES - heads))).astype(BF16)
            proj, gates = _inproj(xf, nw, sc1, sh1, fox_w_in_b, j, fox_main, wg, batch=batch,
                                  q_cols=d, q_scale=FOX_HEAD_DIM ** -0.5 * LOG2E, gates_head_major=False)
            bias = jnp.pad(fox_b_f[j].astype(F32), (0, LANES - heads)).reshape(1, LANES)
            ck_aug = _cumgate(gates, bias, batch=batch, heads=heads, group=FOX_HEADS_PER_STEP)
            y = _fox_attention(proj, ck_aug, batch=batch, d_model=d)
            xf = _outproj(y, fox_w_out_b, j, xf, g1, batch=batch)
        xf = _ffn(xf, norm_ffn_w[layer].reshape(1, d), sc2, sh2, g2, ffn_w_gate_up_b, ffn_w_down_b, layer,
                  final_w, batch=batch, final_norm=(layer == depth - 1))
    return xf.reshape(batch, t, d)
```

```python
import functools

import jax
import jax.numpy as jnp
from jax import lax
from jax.experimental import pallas as pl
from jax.experimental.pallas import tpu as pltpu

F32 = jnp.float32
BF16 = jnp.bfloat16

EPS = 1e-6
GATE_SOFTCAP = 15.0
MLSTM_HEADS = 4
FOX_HEAD_DIM = 128
GATE_ROWS = 16
LANES = 128
LOG2E = 1.4426950408889634

VMEM_LIMIT = 56 * 1024 * 1024

ROW_TILE = 1024
INPROJ_COL_TILE = 2048
OUTPROJ_ROW_TILE = 512
FFN_FF_TILE = 512
MOD_COL_TILE = 1024
MLSTM_CHUNK = 256
CUM_TILE = 512
FOX_BLOCK = 512


def _params(*sem):
    return pltpu.CompilerParams(dimension_semantics=sem, vmem_limit_bytes=VMEM_LIMIT)


def _tile(n, pref):
    t = min(n, pref)
    assert n % t == 0, (n, pref)
    return t


def _log_sigmoid(x):
    return jnp.minimum(x, 0.0) - jnp.log1p(jnp.exp(-jnp.abs(x)))


def _sigmoid(x):
    return 1.0 / (1.0 + jnp.exp(-x))


def _dot(a, b):
    return jnp.dot(a, b, preferred_element_type=F32)


def _dot_nt(a, b):
    return lax.dot_general(a, b, (((1,), (1,)), ((), ())), preferred_element_type=F32)


def _dot_tn(a, b):
    return lax.dot_general(a, b, (((0,), (0,)), ((), ())), preferred_element_type=F32)


def _modulated_norm(x, nw, sc, sh):
    y = x * lax.rsqrt(jnp.mean(x * x, axis=-1, keepdims=True) + EPS)
    return (y * nw) * (1.0 + sc) + sh


def _mod_kernel(c_ref, w_ref, b_ref, o_ref):
    c = c_ref[...]
    cs = (c * _sigmoid(c)).astype(BF16)
    o_ref[0] = _dot(cs, w_ref[0].astype(BF16)) + b_ref[0]


def _modulation(c_pad, ada_w, ada_b):
    depth, d, n6 = ada_w.shape
    rows = c_pad.shape[0]
    tn = _tile(n6, MOD_COL_TILE)
    return pl.pallas_call(
        _mod_kernel,
        out_shape=jax.ShapeDtypeStruct((depth, rows, n6), F32),
        grid=(depth, n6 // tn),
        in_specs=[pl.BlockSpec((rows, d), lambda l, j: (0, 0)),
                  pl.BlockSpec((1, d, tn), lambda l, j: (l, 0, j)),
                  pl.BlockSpec((1, 1, tn), lambda l, j: (l, 0, j))],
        out_specs=pl.BlockSpec((1, rows, tn), lambda l, j: (l, 0, j)),
        compiler_params=_params("parallel", "parallel"),
        name="adaln_modulation",
    )(c_pad, ada_w, ada_b.reshape(depth, 1, n6))


def _inproj_kernel(x_ref, nw_ref, sc_ref, sh_ref, w_ref, wg_ref, o_ref, g_ref, h_ref, *, q_cols, q_scale,
                   gates_head_major):
    j = pl.program_id(1)
    tn = o_ref.shape[1]

    def project(hb):
        col = j * tn + lax.broadcasted_iota(jnp.int32, (1, tn), 1)
        scale = jnp.where(col < q_cols, q_scale, 1.0).astype(F32)
        o_ref[...] = (_dot(hb, w_ref[0]) * scale).astype(o_ref.dtype)

    @pl.when(j == 0)
    def _():
        hb = _modulated_norm(x_ref[...], nw_ref[...], sc_ref[0], sh_ref[0]).astype(BF16)
        h_ref[...] = hb
        project(hb)
        if gates_head_major:
            g_ref[0] = _dot_nt(wg_ref[...], hb)
        else:
            g_ref[...] = _dot(hb, wg_ref[...])

    @pl.when(j > 0)
    def _():
        project(h_ref[...])


def _inproj(x, nw, sc, sh, w, layer, nout, wg, *, batch, q_cols, q_scale, gates_head_major):
    n, d = x.shape
    t = n // batch
    tm = _tile(t, ROW_TILE)
    tn = _tile(nout, INPROJ_COL_TILE)
    tpb = t // tm
    kern = functools.partial(_inproj_kernel, q_cols=q_cols, q_scale=q_scale, gates_head_major=gates_head_major)
    if gates_head_major:
        g_shape = jax.ShapeDtypeStruct((batch, GATE_ROWS, t), F32)
        g_spec = pl.BlockSpec((1, GATE_ROWS, tm), lambda i, j: (i // tpb, 0, i % tpb))
    else:
        g_shape = jax.ShapeDtypeStruct((n, LANES), F32)
        g_spec = pl.BlockSpec((tm, LANES), lambda i, j: (i, 0))
    return pl.pallas_call(
        kern,
        out_shape=(jax.ShapeDtypeStruct((n, nout), BF16), g_shape),
        grid=(n // tm, nout // tn),
        in_specs=[pl.BlockSpec((tm, d), lambda i, j: (i, 0)),
                  pl.BlockSpec((1, d), lambda i, j: (0, 0)),
                  pl.BlockSpec((1, 1, d), lambda i, j: (i // tpb, 0, 0)),
                  pl.BlockSpec((1, 1, d), lambda i, j: (i // tpb, 0, 0)),
                  pl.BlockSpec((1, d, tn), lambda i, j: (layer, 0, j)),
                  pl.BlockSpec(wg.shape, lambda i, j: (0, 0))],
        out_specs=(pl.BlockSpec((tm, tn), lambda i, j: (i, j)), g_spec),
        scratch_shapes=[pltpu.VMEM((tm, d), BF16)],
        compiler_params=_params("parallel", "arbitrary"),
        name="norm_inproj",
    )(x, nw, sc, sh, w, wg)


def _outproj_kernel(y_ref, w_ref, x_ref, g_ref, o_ref):
    o_ref[...] = x_ref[...] + g_ref[0] * _dot(y_ref[...], w_ref[0])


def _outproj(y, w, layer, x, gate, *, batch):
    n, kdim = y.shape
    d = w.shape[2]
    t = n // batch
    tm = _tile(t, OUTPROJ_ROW_TILE)
    tn = d
    tpb = t // tm
    return pl.pallas_call(
        _outproj_kernel,
        out_shape=jax.ShapeDtypeStruct((n, d), F32),
        grid=(n // tm, d // tn),
        in_specs=[pl.BlockSpec((tm, kdim), lambda i, j: (i, 0)),
                  pl.BlockSpec((1, kdim, tn), lambda i, j: (layer, 0, j)),
                  pl.BlockSpec((tm, tn), lambda i, j: (i, j)),
                  pl.BlockSpec((1, 1, tn), lambda i, j: (i // tpb, 0, j))],
        out_specs=pl.BlockSpec((tm, tn), lambda i, j: (i, j)),
        compiler_params=_params("parallel", "arbitrary"),
        name="outproj_residual",
    )(y, w, x, gate)


def _ffn_kernel(x_ref, nw_ref, sc_ref, sh_ref, g2_ref, wg_ref, wu_ref, wd_ref, fw_ref, o_ref, h_ref, *, final_norm):
    f = pl.program_id(1)

    def contribution(h):
        g = _dot(h, wg_ref[0])
        u = _dot(h, wu_ref[0])
        a = (g * _sigmoid(g) * u).astype(BF16)
        return g2_ref[0] * _dot(a, wd_ref[0])

    @pl.when(f == 0)
    def _():
        x = x_ref[...]
        hb = _modulated_norm(x, nw_ref[...], sc_ref[0], sh_ref[0]).astype(BF16)
        h_ref[...] = hb
        o_ref[...] = x + contribution(hb)

    @pl.when(f > 0)
    def _():
        o_ref[...] += contribution(h_ref[...])

    if final_norm:
        @pl.when(f == pl.num_programs(1) - 1)
        def _():
            out = o_ref[...]
            o_ref[...] = out * lax.rsqrt(jnp.mean(out * out, axis=-1, keepdims=True) + EPS) * fw_ref[...]


def _ffn(x, nw, sc, sh, g2, w_gate_up, w_down, layer, final_w, *, batch, final_norm):
    n, d = x.shape
    dff = w_down.shape[1]
    t = n // batch
    tm = _tile(t, ROW_TILE)
    tf = _tile(dff, FFN_FF_TILE)
    tpb = t // tm
    nf = dff // tf
    vec = lambda: pl.BlockSpec((1, 1, d), lambda i, f: (i // tpb, 0, 0))
    return pl.pallas_call(
        functools.partial(_ffn_kernel, final_norm=final_norm),
        out_shape=jax.ShapeDtypeStruct((n, d), F32),
        grid=(n // tm, nf),
        in_specs=[pl.BlockSpec((tm, d), lambda i, f: (i, 0)),
                  pl.BlockSpec((1, d), lambda i, f: (0, 0)),
                  vec(), vec(), vec(),
                  pl.BlockSpec((1, d, tf), lambda i, f: (layer, 0, f)),
                  pl.BlockSpec((1, d, tf), lambda i, f: (layer, 0, nf + f)),
                  pl.BlockSpec((1, tf, d), lambda i, f: (layer, f, 0)),
                  pl.BlockSpec((1, d), lambda i, f: (0, 0))],
        out_specs=pl.BlockSpec((tm, d), lambda i, f: (i, 0)),
        scratch_shapes=[pltpu.VMEM((tm, d), BF16)],
        compiler_params=_params("parallel", "arbitrary"),
        name="swiglu_ffn",
    )(x, nw, sc, sh, g2, w_gate_up, w_gate_up, w_down, final_w)


def _mlstm_kernel(q_ref, k_ref, v_ref, o_ref, g_ref, bias_ref, nw_ref, y_ref, c_ref, n_ref, m_ref, *, heads):
    ci = pl.program_id(1)
    L = q_ref.shape[0]
    dqk = q_ref.shape[1] // heads
    dv = v_ref.shape[1] // heads

    @pl.when(ci == 0)
    def _():
        c_ref[...] = jnp.zeros_like(c_ref)
        n_ref[...] = jnp.zeros_like(n_ref)
        m_ref[...] = jnp.zeros_like(m_ref)

    ones = jnp.ones((L, LANES), BF16)
    ones_dv = jnp.ones((dv, LANES), BF16)
    q = [q_ref[:, h * dqk:(h + 1) * dqk] for h in range(heads)]
    k = [k_ref[:, h * dqk:(h + 1) * dqk] for h in range(heads)]
    qk = [_dot_nt(q[h], k[h]) for h in range(heads)]
    qc = [_dot(q[h], c_ref[h].astype(BF16)) for h in range(heads)]
    qn = [_dot(q[h], n_ref[h].astype(BF16)) for h in range(heads)]

    gates = g_ref[0] + bias_ref[...]
    gates = GATE_SOFTCAP * jnp.tanh(gates / GATE_SOFTCAP)
    row = lax.broadcasted_iota(jnp.int32, (L, L), 0)
    col = lax.broadcasted_iota(jnp.int32, (L, L), 1)
    causal = row >= col
    eye = row == col

    for h in range(heads):
        li_row = gates[h:h + 1, :]
        lf_row = _log_sigmoid(gates[heads + h:heads + h + 1, :])
        b_col = jnp.sum(jnp.where(causal, lf_row, 0.0), axis=1, keepdims=True)
        b_row = jnp.sum(jnp.where(eye, b_col, 0.0), axis=0, keepdims=True)
        li_col = jnp.sum(jnp.where(eye, li_row, 0.0), axis=1, keepdims=True)
        g_tot = jnp.sum(lf_row, axis=1, keepdims=True)

        m_prev = m_ref[h:h + 1, 0:1]
        dmat = jnp.where(causal, b_col + (li_row - b_row), -jnp.inf)
        inter_log = b_col + m_prev
        m_t = jnp.maximum(inter_log, jnp.max(dmat, axis=1, keepdims=True))
        w_intra = jnp.exp(dmat - m_t)
        w_inter = jnp.exp(inter_log - m_t)

        v = v_ref[:, h * dv:(h + 1) * dv]
        s = (qk[h] * w_intra).astype(BF16)
        num = w_inter * qc[h] + _dot(s, v)
        den = w_inter * qn[h][:, 0:1] + _dot(s, ones)[:, 0:1]
        r = 1.0 / jnp.maximum(jnp.abs(den), jnp.exp(-m_t))
        sumsq = _dot((num * num).astype(BF16), ones_dv)[:, 0:1]
        rms_h = r * jnp.sqrt(sumsq * (1.0 / dv))
        row_scale = r * lax.rsqrt(rms_h * rms_h + EPS)

        tail = g_tot - b_col + li_col
        m_new = jnp.maximum(g_tot + m_prev, jnp.max(tail, axis=0, keepdims=True))
        w_tail = jnp.exp(tail - m_new)
        decay = jnp.exp(g_tot + m_prev - m_new)
        kw = (k[h].astype(F32) * w_tail).astype(BF16)
        c_ref[h] = decay * c_ref[h] + _dot_tn(kw, v)
        n_ref[h] = decay * n_ref[h] + _dot_tn(kw, ones)
        m_ref[h:h + 1, :] = jnp.broadcast_to(m_new, (1, LANES))

        half_hs = (num * (0.5 * row_scale) * nw_ref[:, h * dv:(h + 1) * dv]).astype(BF16)
        t = jnp.tanh(o_ref[:, h * dv:(h + 1) * dv] * 0.5)
        y_ref[:, h * dv:(h + 1) * dv] = (half_hs + half_hs * t).astype(y_ref.dtype)


def _mlstm(proj, gates_t, bias, norm_w, *, batch, d_model):
    n = proj.shape[0]
    t = n // batch
    heads = MLSTM_HEADS
    dqk = d_model // (2 * heads)
    dv = d_model // heads
    L = _tile(t, MLSTM_CHUNK)
    nc = t // L
    qk_w = heads * dqk
    assert (2 * qk_w) % (heads * dv) == 0 and heads <= 8
    v_blk = (2 * qk_w) // (heads * dv)
    return pl.pallas_call(
        functools.partial(_mlstm_kernel, heads=heads),
        out_shape=jax.ShapeDtypeStruct((n, heads * dv), BF16),
        grid=(batch, nc),
        in_specs=[pl.BlockSpec((L, qk_w), lambda b, c: (b * nc + c, 0)),
                  pl.BlockSpec((L, qk_w), lambda b, c: (b * nc + c, 1)),
                  pl.BlockSpec((L, heads * dv), lambda b, c: (b * nc + c, v_blk)),
                  pl.BlockSpec((L, heads * dv), lambda b, c: (b * nc + c, v_blk + 1)),
                  pl.BlockSpec((1, GATE_ROWS, L), lambda b, c: (b, 0, c)),
                  pl.BlockSpec((GATE_ROWS, 1), lambda b, c: (0, 0)),
                  pl.BlockSpec((1, heads * dv), lambda b, c: (0, 0))],
        out_specs=pl.BlockSpec((L, heads * dv), lambda b, c: (b * nc + c, 0)),
        scratch_shapes=[pltpu.VMEM((heads, dqk, dv), F32), pltpu.VMEM((heads, dqk, LANES), F32),
                        pltpu.VMEM((8, LANES), F32)],
        compiler_params=_params("parallel", "arbitrary"),
        name="mlstm_chunkwise",
    )(proj, proj, proj, proj, gates_t, bias, norm_w)


BIAS_PIECES = 3


def _split_bf16(x):
    p0 = x.astype(BF16)
    r1 = x - p0.astype(F32)
    p1 = r1.astype(BF16)
    p2 = (r1 - p1.astype(F32)).astype(BF16)
    return p0, p1, p2


def _cumgate_kernel(g_ref, bias_ref, o_ref, carry_ref, *, group):
    @pl.when(pl.program_id(1) == 0)
    def _():
        carry_ref[...] = jnp.zeros_like(carry_ref)

    lf = _log_sigmoid(g_ref[...] + bias_ref[...])
    tc = lf.shape[0]
    row = lax.broadcasted_iota(jnp.int32, (tc, tc), 0)
    col = lax.broadcasted_iota(jnp.int32, (tc, tc), 1)
    lower = jnp.where(row >= col, 1.0, 0.0).astype(BF16)
    cum = carry_ref[0:1, :]
    for piece in _split_bf16(lf):
        cum = cum + _dot(lower, piece)
    carry_ref[...] = jnp.broadcast_to(cum[tc - 1:tc, :], carry_ref.shape)

    width = o_ref.shape[1]
    src = lax.broadcasted_iota(jnp.int32, (LANES, width), 0)
    dst = lax.broadcasted_iota(jnp.int32, (LANES, width), 1)
    dst_head = (dst // LANES) * group + dst % group
    dst_piece = (dst % LANES) // group
    out = jnp.zeros((tc, width), F32)
    for j, piece in enumerate(_split_bf16(cum * LOG2E)):
        sel = jnp.where((dst_head == src) & (dst_piece == j), 1.0, 0.0).astype(BF16)
        out = out + _dot(piece, sel)
    o_ref[...] = out.astype(o_ref.dtype)


def _cumgate(gates, bias, *, batch, heads, group):
    n, lanes = gates.shape
    t = n // batch
    tc = _tile(t, CUM_TILE)
    nc = t // tc
    assert heads % group == 0 and group & (group - 1) == 0 and BIAS_PIECES * group <= LANES
    width = (heads // group) * LANES
    return pl.pallas_call(
        functools.partial(_cumgate_kernel, group=group),
        out_shape=jax.ShapeDtypeStruct((n, width), BF16),
        grid=(batch, nc),
        in_specs=[pl.BlockSpec((tc, lanes), lambda b, c: (b * nc + c, 0)),
                  pl.BlockSpec((1, lanes), lambda b, c: (0, 0))],
        out_specs=pl.BlockSpec((tc, width), lambda b, c: (b * nc + c, 0)),
        scratch_shapes=[pltpu.VMEM((8, lanes), F32)],
        compiler_params=_params("parallel", "arbitrary"),
        name="fox_cum_log_forget",
    )(gates, bias)


def _fox_kernel(q_ref, k_ref, ck_ref, v_ref, o_ref, *scratch, tk):
    per_step = q_ref.shape[0] // tk
    for sub in range(per_step):
        _fox_query_block(pl.program_id(2) * per_step + sub, slice(sub * tk, (sub + 1) * tk),
                         q_ref, k_ref, ck_ref, v_ref, o_ref, *scratch, tk=tk)


def _fox_query_block(qi, rows, q_ref, k_ref, ck_ref, v_ref, o_ref, s_ref, mb_ref, m_ref, l_ref, acc_ref, *, tk):
    tq = tk
    dh = FOX_HEAD_DIM
    nh = q_ref.shape[1] // dh
    lane = lax.broadcasted_iota(jnp.int32, (tq, LANES), 1)
    q_aug = []
    for hd in range(nh):
        minus_one = jnp.where((lane % nh == hd) & (lane < BIAS_PIECES * nh), -1.0, 0.0).astype(BF16)
        q_aug.append(jnp.concatenate([q_ref[rows, hd * dh:(hd + 1) * dh], minus_one], axis=1))

    def head_scores(hd, ki, slot, diagonal=False):
        start = pl.multiple_of(ki * tk, tk)
        k_aug = jnp.concatenate([k_ref[pl.ds(start, tk), hd * dh:(hd + 1) * dh],
                                 ck_ref[pl.ds(start, tk), :]], axis=1)
        if diagonal:
            for c0 in range(0, tq, QCHUNK):
                s_ref[slot, hd, 0:c0 + QCHUNK, c0:c0 + QCHUNK] = _dot_nt(k_aug[0:c0 + QCHUNK],
                                                                           q_aug[hd][c0:c0 + QCHUNK])
        else:
            s = _dot_nt(k_aug, q_aug[hd])
            s_ref[slot, hd] = s
            mb_ref[slot, hd] = jnp.max(s, axis=0, keepdims=True)

    def head_softmax_pv(hd, ki, slot, diagonal, first):
        start = pl.multiple_of(ki * tk, tk)
        for c0 in range(0, tq, QCHUNK):
            qs = slice(c0, c0 + QCHUNK)
            nk = c0 + QCHUNK if diagonal else tk
            v = v_ref[pl.ds(start, nk), hd * dh:(hd + 1) * dh]
            s = s_ref[slot, hd, 0:nk, qs]
            if diagonal:
                key = lax.broadcasted_iota(jnp.int32, (nk, QCHUNK), 0)
                qry = lax.broadcasted_iota(jnp.int32, (nk, QCHUNK), 1) + c0
                s = jnp.where(key <= qry, s, -jnp.inf)
                m_blk = jnp.max(s, axis=0, keepdims=True)
            else:
                m_blk = mb_ref[slot, hd, :, qs]
            m_new = m_blk if first else jnp.maximum(m_ref[hd, :, qs], m_blk)
            p = jnp.exp2(s - m_new)
            l_blk = jnp.sum(p, axis=0, keepdims=True)
            pv = _dot_tn(v, p.astype(BF16))
            if first:
                l_ref[hd, :, qs] = l_blk
                acc_ref[hd, :, qs] = pv
            else:
                alpha = jnp.exp2(m_ref[hd, :, qs] - m_new)
                l_ref[hd, :, qs] = alpha * l_ref[hd, :, qs] + l_blk
                acc_ref[hd, :, qs] = alpha * acc_ref[hd, :, qs] + pv
            m_ref[hd, :, qs] = m_new

    def softmax_pv(ki, slot, diagonal=False, next_block=None, first=False):
        for hd in range(nh):
            if next_block is not None:
                head_scores(hd, *next_block)
            head_softmax_pv(hd, ki, slot, diagonal, first)

    for hd in range(nh):
        head_scores(hd, qi, 0, diagonal=True)
    softmax_pv(qi, 0, diagonal=True, next_block=(0, 1), first=True)

    def pair(j, carry):
        softmax_pv(2 * j, 1, next_block=(2 * j + 1, 0))
        softmax_pv(2 * j + 1, 0, next_block=(2 * j + 2, 1))
        return carry

    lax.fori_loop(0, (qi - 1) // 2, pair, 0)

    @pl.when(qi % 2 == 1)
    def _():
        softmax_pv(qi - 1, 1)

    @pl.when((qi % 2 == 0) & (qi > 0))
    def _():
        softmax_pv(qi - 2, 1, next_block=(qi - 1, 0))
        softmax_pv(qi - 1, 0)

    for hd in range(nh):
        o_ref[rows, hd * dh:(hd + 1) * dh] = (acc_ref[hd] * (1.0 / l_ref[hd])).T.astype(o_ref.dtype)


FOX_HEADS_PER_STEP = 4
Q_BLOCKS_PER_STEP = 2
QCHUNK = 256


def _fox_attention(proj, ck_aug, *, batch, d_model):
    n = proj.shape[0]
    t = n // batch
    dh = FOX_HEAD_DIM
    nh = FOX_HEADS_PER_STEP
    groups = d_model // (nh * dh)
    assert ck_aug.shape[1] == groups * LANES
    tq = _tile(t, FOX_BLOCK)
    rows = _tile(t, Q_BLOCKS_PER_STEP * tq)
    nq = t // rows
    return pl.pallas_call(
        functools.partial(_fox_kernel, tk=tq),
        out_shape=jax.ShapeDtypeStruct((n, d_model), BF16),
        grid=(batch, groups, nq),
        in_specs=[pl.BlockSpec((rows, nh * dh), lambda b, g, i: (b * nq + i, g)),
                  pl.BlockSpec((t, nh * dh), lambda b, g, i: (b, groups + g)),
                  pl.BlockSpec((t, LANES), lambda b, g, i: (b, g)),
                  pl.BlockSpec((t, nh * dh), lambda b, g, i: (b, 2 * groups + g))],
        out_specs=pl.BlockSpec((rows, nh * dh), lambda b, g, i: (b * nq + i, g)),
        scratch_shapes=[pltpu.VMEM((2, nh, tq, tq), F32),
                        pltpu.VMEM((2, nh, 1, tq), F32),
                        pltpu.VMEM((nh, 1, tq), F32),
                        pltpu.VMEM((nh, 1, tq), F32),
                        pltpu.VMEM((nh, dh, tq), F32)],
        compiler_params=_params("parallel", "parallel", "arbitrary"),
        name="fox_attention",
    )(proj, proj, ck_aug, proj)


def _gate_weights_t(w_cols):
    g = w_cols.shape[1]
    return jnp.pad(w_cols.T, ((0, GATE_ROWS - g), (0, 0))).astype(BF16)


def _gate_bias(b):
    return jnp.pad(b.astype(F32), (0, GATE_ROWS - b.shape[0])).reshape(GATE_ROWS, 1)


def kernel(x, c, ada_w, ada_b, norm_mix_w, norm_ffn_w, mlstm_w_in, mlstm_b_gates, mlstm_norm_w, mlstm_w_out,
           fox_w_in, fox_b_f, fox_w_out, ffn_w_gate_up, ffn_w_down, final_norm_w):
    batch, t, d = x.shape
    depth = ada_w.shape[0]
    n = batch * t
    dqk = d // (2 * MLSTM_HEADS)
    mlstm_main = 2 * MLSTM_HEADS * dqk + 2 * d
    fox_main = 3 * d

    c_pad = jnp.pad(c, ((0, GATE_ROWS - batch), (0, 0)))
    mod = _modulation(c_pad, ada_w, ada_b)[:, :batch]
    mod = mod.reshape(depth, batch, 6, 1, d)

    mlstm_w_in_b, mlstm_w_out_b = mlstm_w_in.astype(BF16), mlstm_w_out.astype(BF16)
    fox_w_in_b, fox_w_out_b = fox_w_in.astype(BF16), fox_w_out.astype(BF16)
    ffn_w_gate_up_b, ffn_w_down_b = ffn_w_gate_up.astype(BF16), ffn_w_down.astype(BF16)

    xf = x.reshape(n, d)
    final_w = final_norm_w.reshape(1, d)
    for layer in range(depth):
        sh1, sc1, g1, sh2, sc2, g2 = [mod[layer, :, i] for i in range(6)]
        j = layer // 2
        nw = norm_mix_w[layer].reshape(1, d)
        if layer % 2 == 0:
            proj, gates_t = _inproj(xf, nw, sc1, sh1, mlstm_w_in_b, j, mlstm_main,
                                    _gate_weights_t(mlstm_w_in[j, :, mlstm_main:]), batch=batch,
                                    q_cols=MLSTM_HEADS * dqk, q_scale=dqk ** -0.5, gates_head_major=True)
            y = _mlstm(proj, gates_t, _gate_bias(mlstm_b_gates[j]), mlstm_norm_w[j].reshape(1, d),
                       batch=batch, d_model=d)
            xf = _outproj(y, mlstm_w_out_b, j, xf, g1, batch=batch)
        else:
            heads = d // FOX_HEAD_DIM
            wg = jnp.pad(fox_w_in[j, :, fox_main:], ((0, 0), (0, LANES - heads))).astype(BF16)
            proj, gates = _inproj(xf, nw, sc1, sh1, fox_w_in_b, j, fox_main, wg, batch=batch,
                                  q_cols=d, q_scale=FOX_HEAD_DIM ** -0.5 * LOG2E, gates_head_major=False)
            bias = jnp.pad(fox_b_f[j].astype(F32), (0, LANES - heads)).reshape(1, LANES)
            ck_aug = _cumgate(gates, bias, batch=batch, heads=heads, group=FOX_HEADS_PER_STEP)
            y = _fox_attention(proj, ck_aug, batch=batch, d_model=d)
            xf = _outproj(y, fox_w_out_b, j, xf, g1, batch=batch)
        xf = _ffn(xf, norm_ffn_w[layer].reshape(1, d), sc2, sh2, g2, ffn_w_gate_up_b, ffn_w_down_b, layer,
                  final_w, batch=batch, final_norm=(layer == depth - 1))
    return xf.reshape(batch, t, d)
```

```python
import functools

import jax
import jax.numpy as jnp
from jax import lax
from jax.experimental import pallas as pl
from jax.experimental.pallas import tpu as pltpu

F32 = jnp.float32
BF16 = jnp.bfloat16

EPS = 1e-6
GATE_SOFTCAP = 15.0
MLSTM_HEADS = 4
FOX_HEAD_DIM = 128
GATE_ROWS = 16
LANES = 128
LOG2E = 1.4426950408889634

VMEM_LIMIT = 56 * 1024 * 1024

ROW_TILE = 1024
INPROJ_COL_TILE = 2048
OUTPROJ_ROW_TILE = 512
FFN_FF_TILE = 512
MOD_COL_TILE = 1024
MLSTM_CHUNK = 256
CUM_TILE = 512
FOX_BLOCK = 512


def _params(*sem):
    return pltpu.CompilerParams(dimension_semantics=sem, vmem_limit_bytes=VMEM_LIMIT)


def _tile(n, pref):
    t = min(n, pref)
    assert n % t == 0, (n, pref)
    return t


def _log_sigmoid(x):
    return jnp.minimum(x, 0.0) - jnp.log1p(jnp.exp(-jnp.abs(x)))


def _sigmoid(x):
    return 1.0 / (1.0 + jnp.exp(-x))


def _dot(a, b):
    return jnp.dot(a, b, preferred_element_type=F32)


def _dot_nt(a, b):
    return lax.dot_general(a, b, (((1,), (1,)), ((), ())), preferred_element_type=F32)


def _dot_tn(a, b):
    return lax.dot_general(a, b, (((0,), (0,)), ((), ())), preferred_element_type=F32)


def _modulated_norm(x, nw, sc, sh):
    y = x * lax.rsqrt(jnp.mean(x * x, axis=-1, keepdims=True) + EPS)
    return (y * nw) * (1.0 + sc) + sh


def _mod_kernel(c_ref, w_ref, b_ref, o_ref):
    c = c_ref[...]
    cs = (c * _sigmoid(c)).astype(BF16)
    o_ref[0] = _dot(cs, w_ref[0].astype(BF16)) + b_ref[0]


def _modulation(c_pad, ada_w, ada_b):
    depth, d, n6 = ada_w.shape
    rows = c_pad.shape[0]
    tn = _tile(n6, MOD_COL_TILE)
    return pl.pallas_call(
        _mod_kernel,
        out_shape=jax.ShapeDtypeStruct((depth, rows, n6), F32),
        grid=(depth, n6 // tn),
        in_specs=[pl.BlockSpec((rows, d), lambda l, j: (0, 0)),
                  pl.BlockSpec((1, d, tn), lambda l, j: (l, 0, j)),
                  pl.BlockSpec((1, 1, tn), lambda l, j: (l, 0, j))],
        out_specs=pl.BlockSpec((1, rows, tn), lambda l, j: (l, 0, j)),
        compiler_params=_params("parallel", "parallel"),
        name="adaln_modulation",
    )(c_pad, ada_w, ada_b.reshape(depth, 1, n6))


def _inproj_kernel(x_ref, nw_ref, sc_ref, sh_ref, w_ref, wg_ref, o_ref, g_ref, h_ref, *, q_cols, q_scale,
                   gates_head_major):
    j = pl.program_id(1)
    tn = o_ref.shape[1]

    def project(hb):
        col = j * tn + lax.broadcasted_iota(jnp.int32, (1, tn), 1)
        scale = jnp.where(col < q_cols, q_scale, 1.0).astype(F32)
        o_ref[...] = (_dot(hb, w_ref[0]) * scale).astype(o_ref.dtype)

    @pl.when(j == 0)
    def _():
        hb = _modulated_norm(x_ref[...], nw_ref[...], sc_ref[0], sh_ref[0]).astype(BF16)
        h_ref[...] = hb
        project(hb)
        if gates_head_major:
            g_ref[0] = _dot_nt(wg_ref[...], hb)
        else:
            g_ref[...] = _dot(hb, wg_ref[...])

    @pl.when(j > 0)
    def _():
        project(h_ref[...])


def _inproj(x, nw, sc, sh, w, layer, nout, wg, *, batch, q_cols, q_scale, gates_head_major):
    n, d = x.shape
    t = n // batch
    tm = _tile(t, ROW_TILE)
    tn = _tile(nout, INPROJ_COL_TILE)
    tpb = t // tm
    kern = functools.partial(_inproj_kernel, q_cols=q_cols, q_scale=q_scale, gates_head_major=gates_head_major)
    if gates_head_major:
        g_shape = jax.ShapeDtypeStruct((batch, GATE_ROWS, t), F32)
        g_spec = pl.BlockSpec((1, GATE_ROWS, tm), lambda i, j: (i // tpb, 0, i % tpb))
    else:
        g_shape = jax.ShapeDtypeStruct((n, LANES), F32)
        g_spec = pl.BlockSpec((tm, LANES), lambda i, j: (i, 0))
    return pl.pallas_call(
        kern,
        out_shape=(jax.ShapeDtypeStruct((n, nout), BF16), g_shape),
        grid=(n // tm, nout // tn),
        in_specs=[pl.BlockSpec((tm, d), lambda i, j: (i, 0)),
                  pl.BlockSpec((1, d), lambda i, j: (0, 0)),
                  pl.BlockSpec((1, 1, d), lambda i, j: (i // tpb, 0, 0)),
                  pl.BlockSpec((1, 1, d), lambda i, j: (i // tpb, 0, 0)),
                  pl.BlockSpec((1, d, tn), lambda i, j: (layer, 0, j)),
                  pl.BlockSpec(wg.shape, lambda i, j: (0, 0))],
        out_specs=(pl.BlockSpec((tm, tn), lambda i, j: (i, j)), g_spec),
        scratch_shapes=[pltpu.VMEM((tm, d), BF16)],
        compiler_params=_params("parallel", "arbitrary"),
        name="norm_inproj",
    )(x, nw, sc, sh, w, wg)


def _outproj_kernel(y_ref, w_ref, x_ref, g_ref, o_ref):
    o_ref[...] = x_ref[...] + g_ref[0] * _dot(y_ref[...], w_ref[0])


def _outproj(y, w, layer, x, gate, *, batch):
    n, kdim = y.shape
    d = w.shape[2]
    t = n // batch
    tm = _tile(t, OUTPROJ_ROW_TILE)
    tn = d
    tpb = t // tm
    return pl.pallas_call(
        _outproj_kernel,
        out_shape=jax.ShapeDtypeStruct((n, d), F32),
        grid=(n // tm, d // tn),
        in_specs=[pl.BlockSpec((tm, kdim), lambda i, j: (i, 0)),
                  pl.BlockSpec((1, kdim, tn), lambda i, j: (layer, 0, j)),
                  pl.BlockSpec((tm, tn), lambda i, j: (i, j)),
                  pl.BlockSpec((1, 1, tn), lambda i, j: (i // tpb, 0, j))],
        out_specs=pl.BlockSpec((tm, tn), lambda i, j: (i, j)),
        compiler_params=_params("parallel", "arbitrary"),
        name="outproj_residual",
    )(y, w, x, gate)


def _ffn_kernel(x_ref, nw_ref, sc_ref, sh_ref, g2_ref, wg_ref, wu_ref, wd_ref, fw_ref, o_ref, h_ref, *, final_norm):
    f = pl.program_id(1)

    def contribution(h):
        g = _dot(h, wg_ref[0])
        u = _dot(h, wu_ref[0])
        a = (g * _sigmoid(g) * u).astype(BF16)
        return g2_ref[0] * _dot(a, wd_ref[0])

    @pl.when(f == 0)
    def _():
        x = x_ref[...]
        hb = _modulated_norm(x, nw_ref[...], sc_ref[0], sh_ref[0]).astype(BF16)
        h_ref[...] = hb
        o_ref[...] = x + contribution(hb)

    @pl.when(f > 0)
    def _():
        o_ref[...] += contribution(h_ref[...])

    if final_norm:
        @pl.when(f == pl.num_programs(1) - 1)
        def _():
            out = o_ref[...]
            o_ref[...] = out * lax.rsqrt(jnp.mean(out * out, axis=-1, keepdims=True) + EPS) * fw_ref[...]


def _ffn(x, nw, sc, sh, g2, w_gate_up, w_down, layer, final_w, *, batch, final_norm):
    n, d = x.shape
    dff = w_down.shape[1]
    t = n // batch
    tm = _tile(t, ROW_TILE)
    tf = _tile(dff, FFN_FF_TILE)
    tpb = t // tm
    nf = dff // tf
    vec = lambda: pl.BlockSpec((1, 1, d), lambda i, f: (i // tpb, 0, 0))
    return pl.pallas_call(
        functools.partial(_ffn_kernel, final_norm=final_norm),
        out_shape=jax.ShapeDtypeStruct((n, d), F32),
        grid=(n // tm, nf),
        in_specs=[pl.BlockSpec((tm, d), lambda i, f: (i, 0)),
                  pl.BlockSpec((1, d), lambda i, f: (0, 0)),
                  vec(), vec(), vec(),
                  pl.BlockSpec((1, d, tf), lambda i, f: (layer, 0, f)),
                  pl.BlockSpec((1, d, tf), lambda i, f: (layer, 0, nf + f)),
                  pl.BlockSpec((1, tf, d), lambda i, f: (layer, f, 0)),
                  pl.BlockSpec((1, d), lambda i, f: (0, 0))],
        out_specs=pl.BlockSpec((tm, d), lambda i, f: (i, 0)),
        scratch_shapes=[pltpu.VMEM((tm, d), BF16)],
        compiler_params=_params("parallel", "arbitrary"),
        name="swiglu_ffn",
    )(x, nw, sc, sh, g2, w_gate_up, w_gate_up, w_down, final_w)


def _mlstm_kernel(q_ref, k_ref, v_ref, o_ref, g_ref, bias_ref, nw_ref, y_ref, c_ref, n_ref, m_ref, *, heads):
    ci = pl.program_id(1)
    L = q_ref.shape[0]
    dqk = q_ref.shape[1] // heads
    dv = v_ref.shape[1] // heads

    @pl.when(ci == 0)
    def _():
        c_ref[...] = jnp.zeros_like(c_ref)
        n_ref[...] = jnp.zeros_like(n_ref)
        m_ref[...] = jnp.zeros_like(m_ref)

    ones = jnp.ones((L, LANES), BF16)
    ones_dv = jnp.ones((dv, LANES), BF16)
    q = [q_ref[:, h * dqk:(h + 1) * dqk] for h in range(heads)]
    k = [k_ref[:, h * dqk:(h + 1) * dqk] for h in range(heads)]
    qk = [_dot_nt(q[h], k[h]) for h in range(heads)]
    qc = [_dot(q[h], c_ref[h].astype(BF16)) for h in range(heads)]
    qn = [_dot(q[h], n_ref[h].astype(BF16)) for h in range(heads)]

    gates = g_ref[0] + bias_ref[...]
    gates = GATE_SOFTCAP * jnp.tanh(gates / GATE_SOFTCAP)
    row = lax.broadcasted_iota(jnp.int32, (L, L), 0)
    col = lax.broadcasted_iota(jnp.int32, (L, L), 1)
    causal = row >= col
    eye = row == col

    for h in range(heads):
        li_row = gates[h:h + 1, :]
        lf_row = _log_sigmoid(gates[heads + h:heads + h + 1, :])
        b_col = jnp.sum(jnp.where(causal, lf_row, 0.0), axis=1, keepdims=True)
        b_row = jnp.sum(jnp.where(eye, b_col, 0.0), axis=0, keepdims=True)
        li_col = jnp.sum(jnp.where(eye, li_row, 0.0), axis=1, keepdims=True)
        g_tot = jnp.sum(lf_row, axis=1, keepdims=True)

        m_prev = m_ref[h:h + 1, 0:1]
        dmat = jnp.where(causal, b_col + (li_row - b_row), -jnp.inf)
        inter_log = b_col + m_prev
        m_t = jnp.maximum(inter_log, jnp.max(dmat, axis=1, keepdims=True))
        w_intra = jnp.exp(dmat - m_t)
        w_inter = jnp.exp(inter_log - m_t)

        v = v_ref[:, h * dv:(h + 1) * dv]
        s = (qk[h] * w_intra).astype(BF16)
        num = w_inter * qc[h] + _dot(s, v)
        den = w_inter * qn[h][:, 0:1] + _dot(s, ones)[:, 0:1]
        r = 1.0 / jnp.maximum(jnp.abs(den), jnp.exp(-m_t))
        sumsq = _dot((num * num).astype(BF16), ones_dv)[:, 0:1]
        rms_h = r * jnp.sqrt(sumsq * (1.0 / dv))
        row_scale = r * lax.rsqrt(rms_h * rms_h + EPS)

        tail = g_tot - b_col + li_col
        m_new = jnp.maximum(g_tot + m_prev, jnp.max(tail, axis=0, keepdims=True))
        w_tail = jnp.exp(tail - m_new)
        decay = jnp.exp(g_tot + m_prev - m_new)
        kw = (k[h].astype(F32) * w_tail).astype(BF16)
        c_ref[h] = decay * c_ref[h] + _dot_tn(kw, v)
        n_ref[h] = decay * n_ref[h] + _dot_tn(kw, ones)
        m_ref[h:h + 1, :] = jnp.broadcast_to(m_new, (1, LANES))

        half_hs = (num * (0.5 * row_scale) * nw_ref[:, h * dv:(h + 1) * dv]).astype(BF16)
        t = jnp.tanh(o_ref[:, h * dv:(h + 1) * dv] * 0.5)
        y_ref[:, h * dv:(h + 1) * dv] = (half_hs + half_hs * t).astype(y_ref.dtype)


def _mlstm(proj, gates_t, bias, norm_w, *, batch, d_model):
    n = proj.shape[0]
    t = n // batch
    heads = MLSTM_HEADS
    dqk = d_model // (2 * heads)
    dv = d_model // heads
    L = _tile(t, MLSTM_CHUNK)
    nc = t // L
    qk_w = heads * dqk
    assert (2 * qk_w) % (heads * dv) == 0 and heads <= 8
    v_blk = (2 * qk_w) // (heads * dv)
    return pl.pallas_call(
        functools.partial(_mlstm_kernel, heads=heads),
        out_shape=jax.ShapeDtypeStruct((n, heads * dv), BF16),
        grid=(batch, nc),
        in_specs=[pl.BlockSpec((L, qk_w), lambda b, c: (b * nc + c, 0)),
                  pl.BlockSpec((L, qk_w), lambda b, c: (b * nc + c, 1)),
                  pl.BlockSpec((L, heads * dv), lambda b, c: (b * nc + c, v_blk)),
                  pl.BlockSpec((L, heads * dv), lambda b, c: (b * nc + c, v_blk + 1)),
                  pl.BlockSpec((1, GATE_ROWS, L), lambda b, c: (b, 0, c)),
                  pl.BlockSpec((GATE_ROWS, 1), lambda b, c: (0, 0)),
                  pl.BlockSpec((1, heads * dv), lambda b, c: (0, 0))],
        out_specs=pl.BlockSpec((L, heads * dv), lambda b, c: (b * nc + c, 0)),
        scratch_shapes=[pltpu.VMEM((heads, dqk, dv), F32), pltpu.VMEM((heads, dqk, LANES), F32),
                        pltpu.VMEM((8, LANES), F32)],
        compiler_params=_params("parallel", "arbitrary"),
        name="mlstm_chunkwise",
    )(proj, proj, proj, proj, gates_t, bias, norm_w)


BIAS_PIECES = 3


def _split_bf16(x):
    p0 = x.astype(BF16)
    r1 = x - p0.astype(F32)
    p1 = r1.astype(BF16)
    p2 = (r1 - p1.astype(F32)).astype(BF16)
    return p0, p1, p2


def _cumgate_kernel(g_ref, bias_ref, o_ref, carry_ref, *, group):
    @pl.when(pl.program_id(1) == 0)
    def _():
        carry_ref[...] = jnp.zeros_like(carry_ref)

    lf = _log_sigmoid(g_ref[...] + bias_ref[...])
    tc = lf.shape[0]
    row = lax.broadcasted_iota(jnp.int32, (tc, tc), 0)
    col = lax.broadcasted_iota(jnp.int32, (tc, tc), 1)
    lower = jnp.where(row >= col, 1.0, 0.0).astype(BF16)
    cum = carry_ref[0:1, :]
    for piece in _split_bf16(lf):
        cum = cum + _dot(lower, piece)
    carry_ref[...] = jnp.broadcast_to(cum[tc - 1:tc, :], carry_ref.shape)

    width = o_ref.shape[1]
    src = lax.broadcasted_iota(jnp.int32, (LANES, width), 0)
    dst = lax.broadcasted_iota(jnp.int32, (LANES, width), 1)
    dst_head = (dst // LANES) * group + dst % group
    dst_piece = (dst % LANES) // group
    out = jnp.zeros((tc, width), F32)
    for j, piece in enumerate(_split_bf16(cum * LOG2E)):
        sel = jnp.where((dst_head == src) & (dst_piece == j), 1.0, 0.0).astype(BF16)
        out = out + _dot(piece, sel)
    o_ref[...] = out.astype(o_ref.dtype)


def _cumgate(gates, bias, *, batch, heads, group):
    n, lanes = gates.shape
    t = n // batch
    tc = _tile(t, CUM_TILE)
    nc = t // tc
    assert heads % group == 0 and group & (group - 1) == 0 and BIAS_PIECES * group <= LANES
    width = (heads // group) * LANES
    return pl.pallas_call(
        functools.partial(_cumgate_kernel, group=group),
        out_shape=jax.ShapeDtypeStruct((n, width), BF16),
        grid=(batch, nc),
        in_specs=[pl.BlockSpec((tc, lanes), lambda b, c: (b * nc + c, 0)),
                  pl.BlockSpec((1, lanes), lambda b, c: (0, 0))],
        out_specs=pl.BlockSpec((tc, width), lambda b, c: (b * nc + c, 0)),
        scratch_shapes=[pltpu.VMEM((8, lanes), F32)],
        compiler_params=_params("parallel", "arbitrary"),
        name="fox_cum_log_forget",
    )(gates, bias)


def _fox_kernel(q_ref, k_ref, ck_ref, v_ref, o_ref, *scratch, tk):
    per_step = q_ref.shape[0] // tk
    for sub in range(per_step):
        _fox_query_block(pl.program_id(2) * per_step + sub, slice(sub * tk, (sub + 1) * tk),
                         q_ref, k_ref, ck_ref, v_ref, o_ref, *scratch, tk=tk)


def _fox_query_block(qi, rows, q_ref, k_ref, ck_ref, v_ref, o_ref, s_ref, mb_ref, m_ref, l_ref, acc_ref, *, tk):
    tq = tk
    dh = FOX_HEAD_DIM
    nh = q_ref.shape[1] // dh
    lane = lax.broadcasted_iota(jnp.int32, (tq, LANES), 1)
    q_aug = []
    for hd in range(nh):
        minus_one = jnp.where((lane % nh == hd) & (lane < BIAS_PIECES * nh), -1.0, 0.0).astype(BF16)
        q_aug.append(jnp.concatenate([q_ref[rows, hd * dh:(hd + 1) * dh], minus_one], axis=1))

    def head_scores(hd, ki, slot, diagonal=False):
        start = pl.multiple_of(ki * tk, tk)
        k_aug = jnp.concatenate([k_ref[pl.ds(start, tk), hd * dh:(hd + 1) * dh],
                                 ck_ref[pl.ds(start, tk), :]], axis=1)
        if diagonal:
            for c0 in range(0, tq, QCHUNK):
                s_ref[slot, hd, 0:c0 + QCHUNK, c0:c0 + QCHUNK] = _dot_nt(k_aug[0:c0 + QCHUNK],
                                                                           q_aug[hd][c0:c0 + QCHUNK])
        else:
            for c0 in range(0, tq, QCHUNK):
                qs = slice(c0, c0 + QCHUNK)
                s = _dot_nt(k_aug, q_aug[hd][qs])
                s_ref[slot, hd, :, qs] = s
                mb_ref[slot, hd, :, qs] = jnp.max(s, axis=0, keepdims=True)

    def head_softmax_pv(hd, ki, slot, diagonal, first):
        start = pl.multiple_of(ki * tk, tk)
        for c0 in range(0, tq, QCHUNK):
            qs = slice(c0, c0 + QCHUNK)
            nk = c0 + QCHUNK if diagonal else tk
            v = v_ref[pl.ds(start, nk), hd * dh:(hd + 1) * dh]
            s = s_ref[slot, hd, 0:nk, qs]
            if diagonal:
                key = lax.broadcasted_iota(jnp.int32, (nk, QCHUNK), 0)
                qry = lax.broadcasted_iota(jnp.int32, (nk, QCHUNK), 1) + c0
                s = jnp.where(key <= qry, s, -jnp.inf)
                m_blk = jnp.max(s, axis=0, keepdims=True)
            else:
                m_blk = mb_ref[slot, hd, :, qs]
            m_new = m_blk if first else jnp.maximum(m_ref[hd, :, qs], m_blk)
            p = jnp.exp2(s - m_new)
            l_blk = jnp.sum(p, axis=0, keepdims=True)
            pv = _dot_tn(v, p.astype(BF16))
            if first:
                l_ref[hd, :, qs] = l_blk
                acc_ref[hd, :, qs] = pv
            else:
                alpha = jnp.exp2(m_ref[hd, :, qs] - m_new)
                l_ref[hd, :, qs] = alpha * l_ref[hd, :, qs] + l_blk
                acc_ref[hd, :, qs] = alpha * acc_ref[hd, :, qs] + pv
            m_ref[hd, :, qs] = m_new

    def softmax_pv(ki, slot, diagonal=False, next_block=None, first=False):
        for hd in range(nh):
            if next_block is not None:
                head_scores(hd, *next_block)
            head_softmax_pv(hd, ki, slot, diagonal, first)

    for hd in range(nh):
        head_scores(hd, qi, 0, diagonal=True)
    softmax_pv(qi, 0, diagonal=True, next_block=(0, 1), first=True)

    def pair(j, carry):
        softmax_pv(2 * j, 1, next_block=(2 * j + 1, 0))
        softmax_pv(2 * j + 1, 0, next_block=(2 * j + 2, 1))
        return carry

    lax.fori_loop(0, (qi - 1) // 2, pair, 0)

    @pl.when(qi % 2 == 1)
    def _():
        softmax_pv(qi - 1, 1)

    @pl.when((qi % 2 == 0) & (qi > 0))
    def _():
        softmax_pv(qi - 2, 1, next_block=(qi - 1, 0))
        softmax_pv(qi - 1, 0)

    for hd in range(nh):
        o_ref[rows, hd * dh:(hd + 1) * dh] = (acc_ref[hd] * (1.0 / l_ref[hd])).T.astype(o_ref.dtype)


FOX_HEADS_PER_STEP = 4
Q_BLOCKS_PER_STEP = 2
QCHUNK = 256


def _fox_attention(proj, ck_aug, *, batch, d_model):
    n = proj.shape[0]
    t = n // batch
    dh = FOX_HEAD_DIM
    nh = FOX_HEADS_PER_STEP
    groups = d_model // (nh * dh)
    assert ck_aug.shape[1] == groups * LANES
    tq = _tile(t, FOX_BLOCK)
    rows = _tile(t, Q_BLOCKS_PER_STEP * tq)
    nq = t // rows
    return pl.pallas_call(
        functools.partial(_fox_kernel, tk=tq),
        out_shape=jax.ShapeDtypeStruct((n, d_model), BF16),
        grid=(batch, groups, nq),
        in_specs=[pl.BlockSpec((rows, nh * dh), lambda b, g, i: (b * nq + i, g)),
                  pl.BlockSpec((t, nh * dh), lambda b, g, i: (b, groups + g)),
                  pl.BlockSpec((t, LANES), lambda b, g, i: (b, g)),
                  pl.BlockSpec((t, nh * dh), lambda b, g, i: (b, 2 * groups + g))],
        out_specs=pl.BlockSpec((rows, nh * dh), lambda b, g, i: (b * nq + i, g)),
        scratch_shapes=[pltpu.VMEM((2, nh, tq, tq), F32),
                        pltpu.VMEM((2, nh, 1, tq), F32),
                        pltpu.VMEM((nh, 1, tq), F32),
                        pltpu.VMEM((nh, 1, tq), F32),
                        pltpu.VMEM((nh, dh, tq), F32)],
        compiler_params=_params("parallel", "parallel", "arbitrary"),
        name="fox_attention",
    )(proj, proj, ck_aug, proj)


def _gate_weights_t(w_cols):
    g = w_cols.shape[1]
    return jnp.pad(w_cols.T, ((0, GATE_ROWS - g), (0, 0))).astype(BF16)


def _gate_bias(b):
    return jnp.pad(b.astype(F32), (0, GATE_ROWS - b.shape[0])).reshape(GATE_ROWS, 1)


def kernel(x, c, ada_w, ada_b, norm_mix_w, norm_ffn_w, mlstm_w_in, mlstm_b_gates, mlstm_norm_w, mlstm_w_out,
           fox_w_in, fox_b_f, fox_w_out, ffn_w_gate_up, ffn_w_down, final_norm_w):
    batch, t, d = x.shape
    depth = ada_w.shape[0]
    n = batch * t
    dqk = d // (2 * MLSTM_HEADS)
    mlstm_main = 2 * MLSTM_HEADS * dqk + 2 * d
    fox_main = 3 * d

    c_pad = jnp.pad(c, ((0, GATE_ROWS - batch), (0, 0)))
    mod = _modulation(c_pad, ada_w, ada_b)[:, :batch]
    mod = mod.reshape(depth, batch, 6, 1, d)

    mlstm_w_in_b, mlstm_w_out_b = mlstm_w_in.astype(BF16), mlstm_w_out.astype(BF16)
    fox_w_in_b, fox_w_out_b = fox_w_in.astype(BF16), fox_w_out.astype(BF16)
    ffn_w_gate_up_b, ffn_w_down_b = ffn_w_gate_up.astype(BF16), ffn_w_down.astype(BF16)

    xf = x.reshape(n, d)
    final_w = final_norm_w.reshape(1, d)
    for layer in range(depth):
        sh1, sc1, g1, sh2, sc2, g2 = [mod[layer, :, i] for i in range(6)]
        j = layer // 2
        nw = norm_mix_w[layer].reshape(1, d)
        if layer % 2 == 0:
            proj, gates_t = _inproj(xf, nw, sc1, sh1, mlstm_w_in_b, j, mlstm_main,
                                    _gate_weights_t(mlstm_w_in[j, :, mlstm_main:]), batch=batch,
                                    q_cols=MLSTM_HEADS * dqk, q_scale=dqk ** -0.5, gates_head_major=True)
            y = _mlstm(proj, gates_t, _gate_bias(mlstm_b_gates[j]), mlstm_norm_w[j].reshape(1, d),
                       batch=batch, d_model=d)
            xf = _outproj(y, mlstm_w_out_b, j, xf, g1, batch=batch)
        else:
            heads = d // FOX_HEAD_DIM
            wg = jnp.pad(fox_w_in[j, :, fox_main:], ((0, 0), (0, LANES - heads))).astype(BF16)
            proj, gates = _inproj(xf, nw, sc1, sh1, fox_w_in_b, j, fox_main, wg, batch=batch,
                                  q_cols=d, q_scale=FOX_HEAD_DIM ** -0.5 * LOG2E, gates_head_major=False)
            bias = jnp.pad(fox_b_f[j].astype(F32), (0, LANES - heads)).reshape(1, LANES)
            ck_aug = _cumgate(gates, bias, batch=batch, heads=heads, group=FOX_HEADS_PER_STEP)
            y = _fox_attention(proj, ck_aug, batch=batch, d_model=d)
            xf = _outproj(y, fox_w_out_b, j, xf, g1, batch=batch)
        xf = _ffn(xf, norm_ffn_w[layer].reshape(1, d), sc2, sh2, g2, ffn_w_gate_up_b, ffn_w_down_b, layer,
                  final_w, batch=batch, final_norm=(layer == depth - 1))
    return xf.reshape(batch, t, d)
```

```python
import functools

import jax
import jax.numpy as jnp
from jax import lax
from jax.experimental import pallas as pl
from jax.experimental.pallas import tpu as pltpu

F32 = jnp.float32
BF16 = jnp.bfloat16

EPS = 1e-6
GATE_SOFTCAP = 15.0
MLSTM_HEADS = 4
FOX_HEAD_DIM = 128
GATE_ROWS = 16
LANES = 128
LOG2E = 1.4426950408889634

VMEM_LIMIT = 56 * 1024 * 1024

ROW_TILE = 1024
INPROJ_COL_TILE = 2048
OUTPROJ_ROW_TILE = 512
FFN_FF_TILE = 512
MOD_COL_TILE = 1024
MLSTM_CHUNK = 256
CUM_TILE = 512
FOX_BLOCK = 512


def _params(*sem):
    return pltpu.CompilerParams(dimension_semantics=sem, vmem_limit_bytes=VMEM_LIMIT)


def _tile(n, pref):
    t = min(n, pref)
    assert n % t == 0, (n, pref)
    return t


def _log_sigmoid(x):
    return jnp.minimum(x, 0.0) - jnp.log1p(jnp.exp(-jnp.abs(x)))


def _sigmoid(x):
    return 1.0 / (1.0 + jnp.exp(-x))


def _dot(a, b):
    return jnp.dot(a, b, preferred_element_type=F32)


def _dot_nt(a, b):
    return lax.dot_general(a, b, (((1,), (1,)), ((), ())), preferred_element_type=F32)


def _dot_tn(a, b):
    return lax.dot_general(a, b, (((0,), (0,)), ((), ())), preferred_element_type=F32)


def _modulated_norm(x, nw, sc, sh):
    y = x * lax.rsqrt(jnp.mean(x * x, axis=-1, keepdims=True) + EPS)
    return (y * nw) * (1.0 + sc) + sh


def _mod_kernel(c_ref, w_ref, b_ref, o_ref):
    c = c_ref[...]
    cs = (c * _sigmoid(c)).astype(BF16)
    o_ref[0] = _dot(cs, w_ref[0].astype(BF16)) + b_ref[0]


def _modulation(c_pad, ada_w, ada_b):
    depth, d, n6 = ada_w.shape
    rows = c_pad.shape[0]
    tn = _tile(n6, MOD_COL_TILE)
    return pl.pallas_call(
        _mod_kernel,
        out_shape=jax.ShapeDtypeStruct((depth, rows, n6), F32),
        grid=(depth, n6 // tn),
        in_specs=[pl.BlockSpec((rows, d), lambda l, j: (0, 0)),
                  pl.BlockSpec((1, d, tn), lambda l, j: (l, 0, j)),
                  pl.BlockSpec((1, 1, tn), lambda l, j: (l, 0, j))],
        out_specs=pl.BlockSpec((1, rows, tn), lambda l, j: (l, 0, j)),
        compiler_params=_params("parallel", "parallel"),
        name="adaln_modulation",
    )(c_pad, ada_w, ada_b.reshape(depth, 1, n6))


def _inproj_kernel(x_ref, nw_ref, sc_ref, sh_ref, w_ref, wg_ref, o_ref, g_ref, h_ref, *, q_cols, q_scale,
                   gates_head_major):
    j = pl.program_id(1)
    tn = o_ref.shape[1]

    def project(hb):
        col = j * tn + lax.broadcasted_iota(jnp.int32, (1, tn), 1)
        scale = jnp.where(col < q_cols, q_scale, 1.0).astype(F32)
        o_ref[...] = (_dot(hb, w_ref[0]) * scale).astype(o_ref.dtype)

    @pl.when(j == 0)
    def _():
        hb = _modulated_norm(x_ref[...], nw_ref[...], sc_ref[0], sh_ref[0]).astype(BF16)
        h_ref[...] = hb
        project(hb)
        if gates_head_major:
            g_ref[0] = _dot_nt(wg_ref[...], hb)
        else:
            g_ref[...] = _dot(hb, wg_ref[...])

    @pl.when(j > 0)
    def _():
        project(h_ref[...])


def _inproj(x, nw, sc, sh, w, layer, nout, wg, *, batch, q_cols, q_scale, gates_head_major):
    n, d = x.shape
    t = n // batch
    tm = _tile(t, ROW_TILE)
    tn = _tile(nout, INPROJ_COL_TILE)
    tpb = t // tm
    kern = functools.partial(_inproj_kernel, q_cols=q_cols, q_scale=q_scale, gates_head_major=gates_head_major)
    if gates_head_major:
        g_shape = jax.ShapeDtypeStruct((batch, GATE_ROWS, t), F32)
        g_spec = pl.BlockSpec((1, GATE_ROWS, tm), lambda i, j: (i // tpb, 0, i % tpb))
    else:
        g_shape = jax.ShapeDtypeStruct((n, LANES), F32)
        g_spec = pl.BlockSpec((tm, LANES), lambda i, j: (i, 0))
    return pl.pallas_call(
        kern,
        out_shape=(jax.ShapeDtypeStruct((n, nout), BF16), g_shape),
        grid=(n // tm, nout // tn),
        in_specs=[pl.BlockSpec((tm, d), lambda i, j: (i, 0)),
                  pl.BlockSpec((1, d), lambda i, j: (0, 0)),
                  pl.BlockSpec((1, 1, d), lambda i, j: (i // tpb, 0, 0)),
                  pl.BlockSpec((1, 1, d), lambda i, j: (i // tpb, 0, 0)),
                  pl.BlockSpec((1, d, tn), lambda i, j: (layer, 0, j)),
                  pl.BlockSpec(wg.shape, lambda i, j: (0, 0))],
        out_specs=(pl.BlockSpec((tm, tn), lambda i, j: (i, j)), g_spec),
        scratch_shapes=[pltpu.VMEM((tm, d), BF16)],
        compiler_params=_params("parallel", "arbitrary"),
        name="norm_inproj",
    )(x, nw, sc, sh, w, wg)


def _outproj_kernel(y_ref, w_ref, x_ref, g_ref, o_ref):
    o_ref[...] = x_ref[...] + g_ref[0] * _dot(y_ref[...], w_ref[0])


def _outproj(y, w, layer, x, gate, *, batch):
    n, kdim = y.shape
    d = w.shape[2]
    t = n // batch
    tm = _tile(t, OUTPROJ_ROW_TILE)
    tn = d
    tpb = t // tm
    return pl.pallas_call(
        _outproj_kernel,
        out_shape=jax.ShapeDtypeStruct((n, d), F32),
        grid=(n // tm, d // tn),
        in_specs=[pl.BlockSpec((tm, kdim), lambda i, j: (i, 0)),
                  pl.BlockSpec((1, kdim, tn), lambda i, j: (layer, 0, j)),
                  pl.BlockSpec((tm, tn), lambda i, j: (i, j)),
                  pl.BlockSpec((1, 1, tn), lambda i, j: (i // tpb, 0, j))],
        out_specs=pl.BlockSpec((tm, tn), lambda i, j: (i, j)),
        compiler_params=_params("parallel", "arbitrary"),
        name="outproj_residual",
    )(y, w, x, gate)


def _ffn_kernel(x_ref, nw_ref, sc_ref, sh_ref, g2_ref, wg_ref, wu_ref, wd_ref, fw_ref, o_ref, h_ref, *, final_norm):
    f = pl.program_id(1)

    def contribution(h):
        g = _dot(h, wg_ref[0])
        u = _dot(h, wu_ref[0])
        a = (g * _sigmoid(g) * u).astype(BF16)
        return g2_ref[0] * _dot(a, wd_ref[0])

    @pl.when(f == 0)
    def _():
        x = x_ref[...]
        hb = _modulated_norm(x, nw_ref[...], sc_ref[0], sh_ref[0]).astype(BF16)
        h_ref[...] = hb
        o_ref[...] = x + contribution(hb)

    @pl.when(f > 0)
    def _():
        o_ref[...] += contribution(h_ref[...])

    if final_norm:
        @pl.when(f == pl.num_programs(1) - 1)
        def _():
            out = o_ref[...]
            o_ref[...] = out * lax.rsqrt(jnp.mean(out * out, axis=-1, keepdims=True) + EPS) * fw_ref[...]


def _ffn(x, nw, sc, sh, g2, w_gate_up, w_down, layer, final_w, *, batch, final_norm):
    n, d = x.shape
    dff = w_down.shape[1]
    t = n // batch
    tm = _tile(t, ROW_TILE)
    tf = _tile(dff, FFN_FF_TILE)
    tpb = t // tm
    nf = dff // tf
    vec = lambda: pl.BlockSpec((1, 1, d), lambda i, f: (i // tpb, 0, 0))
    return pl.pallas_call(
        functools.partial(_ffn_kernel, final_norm=final_norm),
        out_shape=jax.ShapeDtypeStruct((n, d), F32),
        grid=(n // tm, nf),
        in_specs=[pl.BlockSpec((tm, d), lambda i, f: (i, 0)),
                  pl.BlockSpec((1, d), lambda i, f: (0, 0)),
                  vec(), vec(), vec(),
                  pl.BlockSpec((1, d, tf), lambda i, f: (layer, 0, f)),
                  pl.BlockSpec((1, d, tf), lambda i, f: (layer, 0, nf + f)),
                  pl.BlockSpec((1, tf, d), lambda i, f: (layer, f, 0)),
                  pl.BlockSpec((1, d), lambda i, f: (0, 0))],
        out_specs=pl.BlockSpec((tm, d), lambda i, f: (i, 0)),
        scratch_shapes=[pltpu.VMEM((tm, d), BF16)],
        compiler_params=_params("parallel", "arbitrary"),
        name="swiglu_ffn",
    )(x, nw, sc, sh, g2, w_gate_up, w_gate_up, w_down, final_w)


def _mlstm_kernel(q_ref, k_ref, v_ref, o_ref, g_ref, bias_ref, nw_ref, y_ref, c_ref, n_ref, m_ref, *, heads):
    ci = pl.program_id(1)
    L = q_ref.shape[0]
    dqk = q_ref.shape[1] // heads
    dv = v_ref.shape[1] // heads

    @pl.when(ci == 0)
    def _():
        c_ref[...] = jnp.zeros_like(c_ref)
        n_ref[...] = jnp.zeros_like(n_ref)
        m_ref[...] = jnp.zeros_like(m_ref)

    ones = jnp.ones((L, LANES), BF16)
    ones_dv = jnp.ones((dv, LANES), BF16)
    q = [q_ref[:, h * dqk:(h + 1) * dqk] for h in range(heads)]
    k = [k_ref[:, h * dqk:(h + 1) * dqk] for h in range(heads)]
    qk = [_dot_nt(q[h], k[h]) for h in range(heads)]
    qc = [_dot(q[h], c_ref[h].astype(BF16)) for h in range(heads)]
    qn = [_dot(q[h], n_ref[h].astype(BF16)) for h in range(heads)]

    gates = g_ref[0] + bias_ref[...]
    gates = GATE_SOFTCAP * jnp.tanh(gates / GATE_SOFTCAP)
    row = lax.broadcasted_iota(jnp.int32, (L, L), 0)
    col = lax.broadcasted_iota(jnp.int32, (L, L), 1)
    causal = row >= col
    eye = row == col

    for h in range(heads):
        li_row = gates[h:h + 1, :]
        lf_row = _log_sigmoid(gates[heads + h:heads + h + 1, :])
        b_col = jnp.sum(jnp.where(causal, lf_row, 0.0), axis=1, keepdims=True)
        b_row = jnp.sum(jnp.where(eye, b_col, 0.0), axis=0, keepdims=True)
        li_col = jnp.sum(jnp.where(eye, li_row, 0.0), axis=1, keepdims=True)
        g_tot = jnp.sum(lf_row, axis=1, keepdims=True)

        m_prev = m_ref[h:h + 1, 0:1]
        dmat = jnp.where(causal, b_col + (li_row - b_row), -jnp.inf)
        inter_log = b_col + m_prev
        m_t = jnp.maximum(inter_log, jnp.max(dmat, axis=1, keepdims=True))
        w_intra = jnp.exp(dmat - m_t)
        w_inter = jnp.exp(inter_log - m_t)

        v = v_ref[:, h * dv:(h + 1) * dv]
        s = (qk[h] * w_intra).astype(BF16)
        num = w_inter * qc[h] + _dot(s, v)
        den = w_inter * qn[h][:, 0:1] + _dot(s, ones)[:, 0:1]
        r = 1.0 / jnp.maximum(jnp.abs(den), jnp.exp(-m_t))
        sumsq = _dot((num * num).astype(BF16), ones_dv)[:, 0:1]
        rms_h = r * jnp.sqrt(sumsq * (1.0 / dv))
        row_scale = r * lax.rsqrt(rms_h * rms_h + EPS)

        tail = g_tot - b_col + li_col
        m_new = jnp.maximum(g_tot + m_prev, jnp.max(tail, axis=0, keepdims=True))
        w_tail = jnp.exp(tail - m_new)
        decay = jnp.exp(g_tot + m_prev - m_new)
        kw = (k[h].astype(F32) * w_tail).astype(BF16)
        c_ref[h] = decay * c_ref[h] + _dot_tn(kw, v)
        n_ref[h] = decay * n_ref[h] + _dot_tn(kw, ones)
        m_ref[h:h + 1, :] = jnp.broadcast_to(m_new, (1, LANES))

        half_hs = (num * (0.5 * row_scale) * nw_ref[:, h * dv:(h + 1) * dv]).astype(BF16)
        t = jnp.tanh(o_ref[:, h * dv:(h + 1) * dv] * 0.5)
        y_ref[:, h * dv:(h + 1) * dv] = (half_hs + half_hs * t).astype(y_ref.dtype)


def _mlstm(proj, gates_t, bias, norm_w, *, batch, d_model):
    n = proj.shape[0]
    t = n // batch
    heads = MLSTM_HEADS
    dqk = d_model // (2 * heads)
    dv = d_model // heads
    L = _tile(t, MLSTM_CHUNK)
    nc = t // L
    qk_w = heads * dqk
    assert (2 * qk_w) % (heads * dv) == 0 and heads <= 8
    v_blk = (2 * qk_w) // (heads * dv)
    return pl.pallas_call(
        functools.partial(_mlstm_kernel, heads=heads),
        out_shape=jax.ShapeDtypeStruct((n, heads * dv), BF16),
        grid=(batch, nc),
        in_specs=[pl.BlockSpec((L, qk_w), lambda b, c: (b * nc + c, 0)),
                  pl.BlockSpec((L, qk_w), lambda b, c: (b * nc + c, 1)),
                  pl.BlockSpec((L, heads * dv), lambda b, c: (b * nc + c, v_blk)),
                  pl.BlockSpec((L, heads * dv), lambda b, c: (b * nc + c, v_blk + 1)),
                  pl.BlockSpec((1, GATE_ROWS, L), lambda b, c: (b, 0, c)),
                  pl.BlockSpec((GATE_ROWS, 1), lambda b, c: (0, 0)),
                  pl.BlockSpec((1, heads * dv), lambda b, c: (0, 0))],
        out_specs=pl.BlockSpec((L, heads * dv), lambda b, c: (b * nc + c, 0)),
        scratch_shapes=[pltpu.VMEM((heads, dqk, dv), F32), pltpu.VMEM((heads, dqk, LANES), F32),
                        pltpu.VMEM((8, LANES), F32)],
        compiler_params=_params("parallel", "arbitrary"),
        name="mlstm_chunkwise",
    )(proj, proj, proj, proj, gates_t, bias, norm_w)


BIAS_PIECES = 3


def _split_bf16(x):
    p0 = x.astype(BF16)
    r1 = x - p0.astype(F32)
    p1 = r1.astype(BF16)
    p2 = (r1 - p1.astype(F32)).astype(BF16)
    return p0, p1, p2


def _cumgate_kernel(g_ref, bias_ref, o_ref, carry_ref, *, group):
    @pl.when(pl.program_id(1) == 0)
    def _():
        carry_ref[...] = jnp.zeros_like(carry_ref)

    lf = _log_sigmoid(g_ref[...] + bias_ref[...])
    tc = lf.shape[0]
    row = lax.broadcasted_iota(jnp.int32, (tc, tc), 0)
    col = lax.broadcasted_iota(jnp.int32, (tc, tc), 1)
    lower = jnp.where(row >= col, 1.0, 0.0).astype(BF16)
    cum = carry_ref[0:1, :]
    for piece in _split_bf16(lf):
        cum = cum + _dot(lower, piece)
    carry_ref[...] = jnp.broadcast_to(cum[tc - 1:tc, :], carry_ref.shape)

    width = o_ref.shape[1]
    src = lax.broadcasted_iota(jnp.int32, (LANES, width), 0)
    dst = lax.broadcasted_iota(jnp.int32, (LANES, width), 1)
    dst_head = (dst // LANES) * group + dst % group
    dst_piece = (dst % LANES) // group
    out = jnp.zeros((tc, width), F32)
    for j, piece in enumerate(_split_bf16(cum * LOG2E)):
        sel = jnp.where((dst_head == src) & (dst_piece == j), 1.0, 0.0).astype(BF16)
        out = out + _dot(piece, sel)
    o_ref[...] = out.astype(o_ref.dtype)


def _cumgate(gates, bias, *, batch, heads, group):
    n, lanes = gates.shape
    t = n // batch
    tc = _tile(t, CUM_TILE)
    nc = t // tc
    assert heads % group == 0 and group & (group - 1) == 0 and BIAS_PIECES * group <= LANES
    width = (heads // group) * LANES
    return pl.pallas_call(
        functools.partial(_cumgate_kernel, group=group),
        out_shape=jax.ShapeDtypeStruct((n, width), BF16),
        grid=(batch, nc),
        in_specs=[pl.BlockSpec((tc, lanes), lambda b, c: (b * nc + c, 0)),
                  pl.BlockSpec((1, lanes), lambda b, c: (0, 0))],
        out_specs=pl.BlockSpec((tc, width), lambda b, c: (b * nc + c, 0)),
        scratch_shapes=[pltpu.VMEM((8, lanes), F32)],
        compiler_params=_params("parallel", "arbitrary"),
        name="fox_cum_log_forget",
    )(gates, bias)


def _fox_kernel(q_ref, k_ref, ck_ref, v_ref, o_ref, *scratch, tk):
    per_step = q_ref.shape[0] // tk
    for sub in range(per_step):
        _fox_query_block(pl.program_id(2) * per_step + sub, slice(sub * tk, (sub + 1) * tk),
                         q_ref, k_ref, ck_ref, v_ref, o_ref, *scratch, tk=tk)


def _fox_query_block(qi, rows, q_ref, k_ref, ck_ref, v_ref, o_ref, s_ref, mb_ref, m_ref, l_ref, acc_ref, *, tk):
    tq = tk
    dh = FOX_HEAD_DIM
    nh = q_ref.shape[1] // dh
    lane = lax.broadcasted_iota(jnp.int32, (tq, LANES), 1)
    q_aug = []
    for hd in range(nh):
        minus_one = jnp.where((lane % nh == hd) & (lane < BIAS_PIECES * nh), -1.0, 0.0).astype(BF16)
        q_aug.append(jnp.concatenate([q_ref[rows, hd * dh:(hd + 1) * dh], minus_one], axis=1))

    def head_scores(hd, ki, slot, diagonal=False):
        start = pl.multiple_of(ki * tk, tk)
        k_aug = jnp.concatenate([k_ref[pl.ds(start, tk), hd * dh:(hd + 1) * dh],
                                 ck_ref[pl.ds(start, tk), :]], axis=1)
        if diagonal:
            for c0 in range(0, tq, QCHUNK):
                s_ref[slot, hd, 0:c0 + QCHUNK, c0:c0 + QCHUNK] = _dot_nt(k_aug[0:c0 + QCHUNK],
                                                                           q_aug[hd][c0:c0 + QCHUNK])
        else:
            for c0 in range(0, tq, QCHUNK):
                qs = slice(c0, c0 + QCHUNK)
                s = _dot_nt(k_aug, q_aug[hd][qs])
                s_ref[slot, hd, :, qs] = s
                mb_ref[slot, hd, :, qs] = jnp.max(s, axis=0, keepdims=True)

    def head_softmax_pv(hd, ki, slot, diagonal, first):
        start = pl.multiple_of(ki * tk, tk)
        for c0 in range(0, tq, QCHUNK):
            qs = slice(c0, c0 + QCHUNK)
            if diagonal:
                nk = c0 + QCHUNK
                s = s_ref[slot, hd, 0:nk, qs]
                key = lax.broadcasted_iota(jnp.int32, (nk, QCHUNK), 0)
                qry = lax.broadcasted_iota(jnp.int32, (nk, QCHUNK), 1) + c0
                s = jnp.where(key <= qry, s, -jnp.inf)
                m_blk = jnp.max(s, axis=0, keepdims=True)
                m_new = m_blk if first else jnp.maximum(m_ref[hd, :, qs], m_blk)
                p = jnp.exp2(s - m_new)
                l_blk = jnp.sum(p, axis=0, keepdims=True)
                pv = _dot_tn(v_ref[pl.ds(start, nk), hd * dh:(hd + 1) * dh], p.astype(BF16))
            else:
                m_blk = mb_ref[slot, hd, :, qs]
                m_new = m_blk if first else jnp.maximum(m_ref[hd, :, qs], m_blk)
                l_blk, pv = 0.0, 0.0
                for k0 in range(0, tk, KCHUNK):
                    p = jnp.exp2(s_ref[slot, hd, k0:k0 + KCHUNK, qs] - m_new)
                    l_blk = l_blk + jnp.sum(p, axis=0, keepdims=True)
                    v = v_ref[pl.ds(start + k0, KCHUNK), hd * dh:(hd + 1) * dh]
                    pv = pv + _dot_tn(v, p.astype(BF16))
            if first:
                l_ref[hd, :, qs] = l_blk
                acc_ref[hd, :, qs] = pv
            else:
                alpha = jnp.exp2(m_ref[hd, :, qs] - m_new)
                l_ref[hd, :, qs] = alpha * l_ref[hd, :, qs] + l_blk
                acc_ref[hd, :, qs] = alpha * acc_ref[hd, :, qs] + pv
            m_ref[hd, :, qs] = m_new

    def softmax_pv(ki, slot, diagonal=False, next_block=None, first=False):
        for hd in range(nh):
            if next_block is not None:
                head_scores(hd, *next_block)
            head_softmax_pv(hd, ki, slot, diagonal, first)

    for hd in range(nh):
        head_scores(hd, qi, 0, diagonal=True)
    softmax_pv(qi, 0, diagonal=True, next_block=(0, 1), first=True)

    def pair(j, carry):
        softmax_pv(2 * j, 1, next_block=(2 * j + 1, 0))
        softmax_pv(2 * j + 1, 0, next_block=(2 * j + 2, 1))
        return carry

    lax.fori_loop(0, (qi - 1) // 2, pair, 0)

    @pl.when(qi % 2 == 1)
    def _():
        softmax_pv(qi - 1, 1)

    @pl.when((qi % 2 == 0) & (qi > 0))
    def _():
        softmax_pv(qi - 2, 1, next_block=(qi - 1, 0))
        softmax_pv(qi - 1, 0)

    for hd in range(nh):
        o_ref[rows, hd * dh:(hd + 1) * dh] = (acc_ref[hd] * (1.0 / l_ref[hd])).T.astype(o_ref.dtype)


FOX_HEADS_PER_STEP = 4
Q_BLOCKS_PER_STEP = 2
QCHUNK = 256
KCHUNK = 256


def _fox_attention(proj, ck_aug, *, batch, d_model):
    n = proj.shape[0]
    t = n // batch
    dh = FOX_HEAD_DIM
    nh = FOX_HEADS_PER_STEP
    groups = d_model // (nh * dh)
    assert ck_aug.shape[1] == groups * LANES
    tq = _tile(t, FOX_BLOCK)
    rows = _tile(t, Q_BLOCKS_PER_STEP * tq)
    nq = t // rows
    return pl.pallas_call(
        functools.partial(_fox_kernel, tk=tq),
        out_shape=jax.ShapeDtypeStruct((n, d_model), BF16),
        grid=(batch, groups, nq),
        in_specs=[pl.BlockSpec((rows, nh * dh), lambda b, g, i: (b * nq + i, g)),
                  pl.BlockSpec((t, nh * dh), lambda b, g, i: (b, groups + g)),
                  pl.BlockSpec((t, LANES), lambda b, g, i: (b, g)),
                  pl.BlockSpec((t, nh * dh), lambda b, g, i: (b, 2 * groups + g))],
        out_specs=pl.BlockSpec((rows, nh * dh), lambda b, g, i: (b * nq + i, g)),
        scratch_shapes=[pltpu.VMEM((2, nh, tq, tq), F32),
                        pltpu.VMEM((2, nh, 1, tq), F32),
                        pltpu.VMEM((nh, 1, tq), F32),
                        pltpu.VMEM((nh, 1, tq), F32),
                        pltpu.VMEM((nh, dh, tq), F32)],
        compiler_params=_params("parallel", "parallel", "arbitrary"),
        name="fox_attention",
    )(proj, proj, ck_aug, proj)


def _gate_weights_t(w_cols):
    g = w_cols.shape[1]
    return jnp.pad(w_cols.T, ((0, GATE_ROWS - g), (0, 0))).astype(BF16)


def _gate_bias(b):
    return jnp.pad(b.astype(F32), (0, GATE_ROWS - b.shape[0])).reshape(GATE_ROWS, 1)


def kernel(x, c, ada_w, ada_b, norm_mix_w, norm_ffn_w, mlstm_w_in, mlstm_b_gates, mlstm_norm_w, mlstm_w_out,
           fox_w_in, fox_b_f, fox_w_out, ffn_w_gate_up, ffn_w_down, final_norm_w):
    batch, t, d = x.shape
    depth = ada_w.shape[0]
    n = batch * t
    dqk = d // (2 * MLSTM_HEADS)
    mlstm_main = 2 * MLSTM_HEADS * dqk + 2 * d
    fox_main = 3 * d

    c_pad = jnp.pad(c, ((0, GATE_ROWS - batch), (0, 0)))
    mod = _modulation(c_pad, ada_w, ada_b)[:, :batch]
    mod = mod.reshape(depth, batch, 6, 1, d)

    mlstm_w_in_b, mlstm_w_out_b = mlstm_w_in.astype(BF16), mlstm_w_out.astype(BF16)
    fox_w_in_b, fox_w_out_b = fox_w_in.astype(BF16), fox_w_out.astype(BF16)
    ffn_w_gate_up_b, ffn_w_down_b = ffn_w_gate_up.astype(BF16), ffn_w_down.astype(BF16)

    xf = x.reshape(n, d)
    final_w = final_norm_w.reshape(1, d)
    for layer in range(depth):
        sh1, sc1, g1, sh2, sc2, g2 = [mod[layer, :, i] for i in range(6)]
        j = layer // 2
        nw = norm_mix_w[layer].reshape(1, d)
        if layer % 2 == 0:
            proj, gates_t = _inproj(xf, nw, sc1, sh1, mlstm_w_in_b, j, mlstm_main,
                                    _gate_weights_t(mlstm_w_in[j, :, mlstm_main:]), batch=batch,
                                    q_cols=MLSTM_HEADS * dqk, q_scale=dqk ** -0.5, gates_head_major=True)
            y = _mlstm(proj, gates_t, _gate_bias(mlstm_b_gates[j]), mlstm_norm_w[j].reshape(1, d),
                       batch=batch, d_model=d)
            xf = _outproj(y, mlstm_w_out_b, j, xf, g1, batch=batch)
        else:
            heads = d // FOX_HEAD_DIM
            wg = jnp.pad(fox_w_in[j, :, fox_main:], ((0, 0), (0, LANES - heads))).astype(BF16)
            proj, gates = _inproj(xf, nw, sc1, sh1, fox_w_in_b, j, fox_main, wg, batch=batch,
                                  q_cols=d, q_scale=FOX_HEAD_DIM ** -0.5 * LOG2E, gates_head_major=False)
            bias = jnp.pad(fox_b_f[j].astype(F32), (0, LANES - heads)).reshape(1, LANES)
            ck_aug = _cumgate(gates, bias, batch=batch, heads=heads, group=FOX_HEADS_PER_STEP)
            y = _fox_attention(proj, ck_aug, batch=batch, d_model=d)
            xf = _outproj(y, fox_w_out_b, j, xf, g1, batch=batch)
        xf = _ffn(xf, norm_ffn_w[layer].reshape(1, d), sc2, sh2, g2, ffn_w_gate_up_b, ffn_w_down_b, layer,
                  final_w, batch=batch, final_norm=(layer == depth - 1))
    return xf.reshape(batch, t, d)
```

```python
import functools

import jax
import jax.numpy as jnp
from jax import lax
from jax.experimental import pallas as pl
from jax.experimental.pallas import tpu as pltpu

F32 = jnp.float32
BF16 = jnp.bfloat16

EPS = 1e-6
GATE_SOFTCAP = 15.0
MLSTM_HEADS = 4
FOX_HEAD_DIM = 128
GATE_ROWS = 16
LANES = 128
LOG2E = 1.4426950408889634

VMEM_LIMIT = 56 * 1024 * 1024

ROW_TILE = 1024
INPROJ_COL_TILE = 2048
OUTPROJ_ROW_TILE = 1024
FFN_FF_TILE = 512
MOD_COL_TILE = 1024
MLSTM_CHUNK = 256
CUM_TILE = 512
FOX_BLOCK = 512


def _params(*sem):
    return pltpu.CompilerParams(dimension_semantics=sem, vmem_limit_bytes=VMEM_LIMIT)


def _tile(n, pref):
    t = min(n, pref)
    assert n % t == 0, (n, pref)
    return t


def _log_sigmoid(x):
    return jnp.minimum(x, 0.0) - jnp.log1p(jnp.exp(-jnp.abs(x)))


def _sigmoid(x):
    return 1.0 / (1.0 + jnp.exp(-x))


def _dot(a, b):
    return jnp.dot(a, b, preferred_element_type=F32)


def _dot_nt(a, b):
    return lax.dot_general(a, b, (((1,), (1,)), ((), ())), preferred_element_type=F32)


def _dot_tn(a, b):
    return lax.dot_general(a, b, (((0,), (0,)), ((), ())), preferred_element_type=F32)


def _modulated_norm(x, nw, sc, sh):
    y = x * lax.rsqrt(jnp.mean(x * x, axis=-1, keepdims=True) + EPS)
    return (y * nw) * (1.0 + sc) + sh


def _mod_kernel(c_ref, w_ref, b_ref, o_ref):
    c = c_ref[...]
    cs = (c * _sigmoid(c)).astype(BF16)
    o_ref[0] = _dot(cs, w_ref[0].astype(BF16)) + b_ref[0]


def _modulation(c_pad, ada_w, ada_b):
    depth, d, n6 = ada_w.shape
    rows = c_pad.shape[0]
    tn = _tile(n6, MOD_COL_TILE)
    return pl.pallas_call(
        _mod_kernel,
        out_shape=jax.ShapeDtypeStruct((depth, rows, n6), F32),
        grid=(depth, n6 // tn),
        in_specs=[pl.BlockSpec((rows, d), lambda l, j: (0, 0)),
                  pl.BlockSpec((1, d, tn), lambda l, j: (l, 0, j)),
                  pl.BlockSpec((1, 1, tn), lambda l, j: (l, 0, j))],
        out_specs=pl.BlockSpec((1, rows, tn), lambda l, j: (l, 0, j)),
        compiler_params=_params("parallel", "parallel"),
        name="adaln_modulation",
    )(c_pad, ada_w, ada_b.reshape(depth, 1, n6))


def _inproj_kernel(x_ref, nw_ref, sc_ref, sh_ref, w_ref, wg_ref, o_ref, g_ref, h_ref, *, q_cols, q_scale,
                   gates_head_major):
    j = pl.program_id(1)
    tn = o_ref.shape[1]

    def project(hb):
        col = j * tn + lax.broadcasted_iota(jnp.int32, (1, tn), 1)
        scale = jnp.where(col < q_cols, q_scale, 1.0).astype(F32)
        o_ref[...] = (_dot(hb, w_ref[0]) * scale).astype(o_ref.dtype)

    @pl.when(j == 0)
    def _():
        hb = _modulated_norm(x_ref[...], nw_ref[...], sc_ref[0], sh_ref[0]).astype(BF16)
        h_ref[...] = hb
        project(hb)
        if gates_head_major:
            g_ref[0] = _dot_nt(wg_ref[...], hb)
        else:
            g_ref[...] = _dot(hb, wg_ref[...])

    @pl.when(j > 0)
    def _():
        project(h_ref[...])


def _inproj(x, nw, sc, sh, w, layer, nout, wg, *, batch, q_cols, q_scale, gates_head_major):
    n, d = x.shape
    t = n // batch
    tm = _tile(t, ROW_TILE)
    tn = _tile(nout, INPROJ_COL_TILE)
    tpb = t // tm
    kern = functools.partial(_inproj_kernel, q_cols=q_cols, q_scale=q_scale, gates_head_major=gates_head_major)
    if gates_head_major:
        g_shape = jax.ShapeDtypeStruct((batch, GATE_ROWS, t), F32)
        g_spec = pl.BlockSpec((1, GATE_ROWS, tm), lambda i, j: (i // tpb, 0, i % tpb))
    else:
        g_shape = jax.ShapeDtypeStruct((n, LANES), F32)
        g_spec = pl.BlockSpec((tm, LANES), lambda i, j: (i, 0))
    return pl.pallas_call(
        kern,
        out_shape=(jax.ShapeDtypeStruct((n, nout), BF16), g_shape),
        grid=(n // tm, nout // tn),
        in_specs=[pl.BlockSpec((tm, d), lambda i, j: (i, 0)),
                  pl.BlockSpec((1, d), lambda i, j: (0, 0)),
                  pl.BlockSpec((1, 1, d), lambda i, j: (i // tpb, 0, 0)),
                  pl.BlockSpec((1, 1, d), lambda i, j: (i // tpb, 0, 0)),
                  pl.BlockSpec((1, d, tn), lambda i, j: (layer, 0, j)),
                  pl.BlockSpec(wg.shape, lambda i, j: (0, 0))],
        out_specs=(pl.BlockSpec((tm, tn), lambda i, j: (i, j)), g_spec),
        scratch_shapes=[pltpu.VMEM((tm, d), BF16)],
        compiler_params=_params("parallel", "arbitrary"),
        name="norm_inproj",
    )(x, nw, sc, sh, w, wg)


def _outproj_kernel(y_ref, w_ref, x_ref, g_ref, o_ref):
    o_ref[...] = x_ref[...] + g_ref[0] * _dot(y_ref[...], w_ref[0])


def _outproj(y, w, layer, x, gate, *, batch):
    n, kdim = y.shape
    d = w.shape[2]
    t = n // batch
    tm = _tile(t, OUTPROJ_ROW_TILE)
    tn = d
    tpb = t // tm
    return pl.pallas_call(
        _outproj_kernel,
        out_shape=jax.ShapeDtypeStruct((n, d), F32),
        grid=(n // tm, d // tn),
        in_specs=[pl.BlockSpec((tm, kdim), lambda i, j: (i, 0)),
                  pl.BlockSpec((1, kdim, tn), lambda i, j: (layer, 0, j), pipeline_mode=pl.Buffered(1)),
                  pl.BlockSpec((tm, tn), lambda i, j: (i, j)),
                  pl.BlockSpec((1, 1, tn), lambda i, j: (i // tpb, 0, j))],
        out_specs=pl.BlockSpec((tm, tn), lambda i, j: (i, j)),
        compiler_params=_params("parallel", "arbitrary"),
        name="outproj_residual",
    )(y, w, x, gate)


def _ffn_kernel(x_ref, nw_ref, sc_ref, sh_ref, g2_ref, wg_ref, wu_ref, wd_ref, fw_ref, o_ref, h_ref, *, final_norm):
    f = pl.program_id(1)

    def contribution(h):
        g = _dot(h, wg_ref[0])
        u = _dot(h, wu_ref[0])
        a = (g * _sigmoid(g) * u).astype(BF16)
        return g2_ref[0] * _dot(a, wd_ref[0])

    @pl.when(f == 0)
    def _():
        x = x_ref[...]
        hb = _modulated_norm(x, nw_ref[...], sc_ref[0], sh_ref[0]).astype(BF16)
        h_ref[...] = hb
        o_ref[...] = x + contribution(hb)

    @pl.when(f > 0)
    def _():
        o_ref[...] += contribution(h_ref[...])

    if final_norm:
        @pl.when(f == pl.num_programs(1) - 1)
        def _():
            out = o_ref[...]
            o_ref[...] = out * lax.rsqrt(jnp.mean(out * out, axis=-1, keepdims=True) + EPS) * fw_ref[...]


def _ffn(x, nw, sc, sh, g2, w_gate_up, w_down, layer, final_w, *, batch, final_norm):
    n, d = x.shape
    dff = w_down.shape[1]
    t = n // batch
    tm = _tile(t, ROW_TILE)
    tf = _tile(dff, FFN_FF_TILE)
    tpb = t // tm
    nf = dff // tf
    vec = lambda: pl.BlockSpec((1, 1, d), lambda i, f: (i // tpb, 0, 0))
    return pl.pallas_call(
        functools.partial(_ffn_kernel, final_norm=final_norm),
        out_shape=jax.ShapeDtypeStruct((n, d), F32),
        grid=(n // tm, nf),
        in_specs=[pl.BlockSpec((tm, d), lambda i, f: (i, 0)),
                  pl.BlockSpec((1, d), lambda i, f: (0, 0)),
                  vec(), vec(), vec(),
                  pl.BlockSpec((1, d, tf), lambda i, f: (layer, 0, f)),
                  pl.BlockSpec((1, d, tf), lambda i, f: (layer, 0, nf + f)),
                  pl.BlockSpec((1, tf, d), lambda i, f: (layer, f, 0)),
                  pl.BlockSpec((1, d), lambda i, f: (0, 0))],
        out_specs=pl.BlockSpec((tm, d), lambda i, f: (i, 0)),
        scratch_shapes=[pltpu.VMEM((tm, d), BF16)],
        compiler_params=_params("parallel", "arbitrary"),
        name="swiglu_ffn",
    )(x, nw, sc, sh, g2, w_gate_up, w_gate_up, w_down, final_w)


def _mlstm_kernel(q_ref, k_ref, v_ref, o_ref, g_ref, bias_ref, nw_ref, y_ref, c_ref, n_ref, m_ref, *, heads):
    ci = pl.program_id(1)
    L = q_ref.shape[0]
    dqk = q_ref.shape[1] // heads
    dv = v_ref.shape[1] // heads

    @pl.when(ci == 0)
    def _():
        c_ref[...] = jnp.zeros_like(c_ref)
        n_ref[...] = jnp.zeros_like(n_ref)
        m_ref[...] = jnp.zeros_like(m_ref)

    ones = jnp.ones((L, LANES), BF16)
    ones_dv = jnp.ones((dv, LANES), BF16)
    q = [q_ref[:, h * dqk:(h + 1) * dqk] for h in range(heads)]
    k = [k_ref[:, h * dqk:(h + 1) * dqk] for h in range(heads)]
    qk = [_dot_nt(q[h], k[h]) for h in range(heads)]
    qc = [_dot(q[h], c_ref[h].astype(BF16)) for h in range(heads)]
    qn = [_dot(q[h], n_ref[h].astype(BF16)) for h in range(heads)]

    gates = g_ref[0] + bias_ref[...]
    gates = GATE_SOFTCAP * jnp.tanh(gates / GATE_SOFTCAP)
    row = lax.broadcasted_iota(jnp.int32, (L, L), 0)
    col = lax.broadcasted_iota(jnp.int32, (L, L), 1)
    causal = row >= col
    eye = row == col

    for h in range(heads):
        li_row = gates[h:h + 1, :]
        lf_row = _log_sigmoid(gates[heads + h:heads + h + 1, :])
        b_col = jnp.sum(jnp.where(causal, lf_row, 0.0), axis=1, keepdims=True)
        b_row = jnp.sum(jnp.where(eye, b_col, 0.0), axis=0, keepdims=True)
        li_col = jnp.sum(jnp.where(eye, li_row, 0.0), axis=1, keepdims=True)
        g_tot = jnp.sum(lf_row, axis=1, keepdims=True)

        m_prev = m_ref[h:h + 1, 0:1]
        dmat = jnp.where(causal, b_col + (li_row - b_row), -jnp.inf)
        inter_log = b_col + m_prev
        m_t = jnp.maximum(inter_log, jnp.max(dmat, axis=1, keepdims=True))
        w_intra = jnp.exp(dmat - m_t)
        w_inter = jnp.exp(inter_log - m_t)

        v = v_ref[:, h * dv:(h + 1) * dv]
        s = (qk[h] * w_intra).astype(BF16)
        num = w_inter * qc[h] + _dot(s, v)
        den = w_inter * qn[h][:, 0:1] + _dot(s, ones)[:, 0:1]
        r = 1.0 / jnp.maximum(jnp.abs(den), jnp.exp(-m_t))
        sumsq = _dot((num * num).astype(BF16), ones_dv)[:, 0:1]
        rms_h = r * jnp.sqrt(sumsq * (1.0 / dv))
        row_scale = r * lax.rsqrt(rms_h * rms_h + EPS)

        tail = g_tot - b_col + li_col
        m_new = jnp.maximum(g_tot + m_prev, jnp.max(tail, axis=0, keepdims=True))
        w_tail = jnp.exp(tail - m_new)
        decay = jnp.exp(g_tot + m_prev - m_new)
        kw = (k[h].astype(F32) * w_tail).astype(BF16)
        c_ref[h] = decay * c_ref[h] + _dot_tn(kw, v)
        n_ref[h] = decay * n_ref[h] + _dot_tn(kw, ones)
        m_ref[h:h + 1, :] = jnp.broadcast_to(m_new, (1, LANES))

        half_hs = (num * (0.5 * row_scale) * nw_ref[:, h * dv:(h + 1) * dv]).astype(BF16)
        t = jnp.tanh(o_ref[:, h * dv:(h + 1) * dv] * 0.5)
        y_ref[:, h * dv:(h + 1) * dv] = (half_hs + half_hs * t).astype(y_ref.dtype)


def _mlstm(proj, gates_t, bias, norm_w, *, batch, d_model):
    n = proj.shape[0]
    t = n // batch
    heads = MLSTM_HEADS
    dqk = d_model // (2 * heads)
    dv = d_model // heads
    L = _tile(t, MLSTM_CHUNK)
    nc = t // L
    qk_w = heads * dqk
    assert (2 * qk_w) % (heads * dv) == 0 and heads <= 8
    v_blk = (2 * qk_w) // (heads * dv)
    return pl.pallas_call(
        functools.partial(_mlstm_kernel, heads=heads),
        out_shape=jax.ShapeDtypeStruct((n, heads * dv), BF16),
        grid=(batch, nc),
        in_specs=[pl.BlockSpec((L, qk_w), lambda b, c: (b * nc + c, 0)),
                  pl.BlockSpec((L, qk_w), lambda b, c: (b * nc + c, 1)),
                  pl.BlockSpec((L, heads * dv), lambda b, c: (b * nc + c, v_blk)),
                  pl.BlockSpec((L, heads * dv), lambda b, c: (b * nc + c, v_blk + 1)),
                  pl.BlockSpec((1, GATE_ROWS, L), lambda b, c: (b, 0, c)),
                  pl.BlockSpec((GATE_ROWS, 1), lambda b, c: (0, 0)),
                  pl.BlockSpec((1, heads * dv), lambda b, c: (0, 0))],
        out_specs=pl.BlockSpec((L, heads * dv), lambda b, c: (b * nc + c, 0)),
        scratch_shapes=[pltpu.VMEM((heads, dqk, dv), F32), pltpu.VMEM((heads, dqk, LANES), F32),
                        pltpu.VMEM((8, LANES), F32)],
        compiler_params=_params("parallel", "arbitrary"),
        name="mlstm_chunkwise",
    )(proj, proj, proj, proj, gates_t, bias, norm_w)


BIAS_PIECES = 3


def _split_bf16(x):
    p0 = x.astype(BF16)
    r1 = x - p0.astype(F32)
    p1 = r1.astype(BF16)
    p2 = (r1 - p1.astype(F32)).astype(BF16)
    return p0, p1, p2


def _cumgate_kernel(g_ref, bias_ref, o_ref, carry_ref, *, group):
    @pl.when(pl.program_id(1) == 0)
    def _():
        carry_ref[...] = jnp.zeros_like(carry_ref)

    lf = _log_sigmoid(g_ref[...] + bias_ref[...])
    tc = lf.shape[0]
    row = lax.broadcasted_iota(jnp.int32, (tc, tc), 0)
    col = lax.broadcasted_iota(jnp.int32, (tc, tc), 1)
    lower = jnp.where(row >= col, 1.0, 0.0).astype(BF16)
    cum = carry_ref[0:1, :]
    for piece in _split_bf16(lf):
        cum = cum + _dot(lower, piece)
    carry_ref[...] = jnp.broadcast_to(cum[tc - 1:tc, :], carry_ref.shape)

    width = o_ref.shape[1]
    src = lax.broadcasted_iota(jnp.int32, (LANES, width), 0)
    dst = lax.broadcasted_iota(jnp.int32, (LANES, width), 1)
    dst_head = (dst // LANES) * group + dst % group
    dst_piece = (dst % LANES) // group
    out = jnp.zeros((tc, width), F32)
    for j, piece in enumerate(_split_bf16(cum * LOG2E)):
        sel = jnp.where((dst_head == src) & (dst_piece == j), 1.0, 0.0).astype(BF16)
        out = out + _dot(piece, sel)
    o_ref[...] = out.astype(o_ref.dtype)


def _cumgate(gates, bias, *, batch, heads, group):
    n, lanes = gates.shape
    t = n // batch
    tc = _tile(t, CUM_TILE)
    nc = t // tc
    assert heads % group == 0 and group & (group - 1) == 0 and BIAS_PIECES * group <= LANES
    width = (heads // group) * LANES
    return pl.pallas_call(
        functools.partial(_cumgate_kernel, group=group),
        out_shape=jax.ShapeDtypeStruct((n, width), BF16),
        grid=(batch, nc),
        in_specs=[pl.BlockSpec((tc, lanes), lambda b, c: (b * nc + c, 0)),
                  pl.BlockSpec((1, lanes), lambda b, c: (0, 0))],
        out_specs=pl.BlockSpec((tc, width), lambda b, c: (b * nc + c, 0)),
        scratch_shapes=[pltpu.VMEM((8, lanes), F32)],
        compiler_params=_params("parallel", "arbitrary"),
        name="fox_cum_log_forget",
    )(gates, bias)


def _fox_kernel(q_ref, k_ref, ck_ref, v_ref, o_ref, *scratch, tk):
    per_step = q_ref.shape[0] // tk
    for sub in range(per_step):
        _fox_query_block(pl.program_id(2) * per_step + sub, slice(sub * tk, (sub + 1) * tk),
                         q_ref, k_ref, ck_ref, v_ref, o_ref, *scratch, tk=tk)


def _fox_query_block(qi, rows, q_ref, k_ref, ck_ref, v_ref, o_ref, s_ref, mb_ref, m_ref, l_ref, acc_ref, *, tk):
    tq = tk
    dh = FOX_HEAD_DIM
    nh = q_ref.shape[1] // dh
    lane = lax.broadcasted_iota(jnp.int32, (tq, LANES), 1)
    q_aug = []
    for hd in range(nh):
        minus_one = jnp.where((lane % nh == hd) & (lane < BIAS_PIECES * nh), -1.0, 0.0).astype(BF16)
        q_aug.append(jnp.concatenate([q_ref[rows, hd * dh:(hd + 1) * dh], minus_one], axis=1))

    def head_scores(hd, ki, slot, diagonal=False):
        start = pl.multiple_of(ki * tk, tk)
        k_aug = jnp.concatenate([k_ref[pl.ds(start, tk), hd * dh:(hd + 1) * dh],
                                 ck_ref[pl.ds(start, tk), :]], axis=1)
        if diagonal:
            for c0 in range(0, tq, QCHUNK):
                s_ref[slot, hd, 0:c0 + QCHUNK, c0:c0 + QCHUNK] = _dot_nt(k_aug[0:c0 + QCHUNK],
                                                                           q_aug[hd][c0:c0 + QCHUNK])
        else:
            for c0 in range(0, tq, QCHUNK):
                qs = slice(c0, c0 + QCHUNK)
                s = _dot_nt(k_aug, q_aug[hd][qs])
                s_ref[slot, hd, :, qs] = s
                mb_ref[slot, hd, :, qs] = jnp.max(s, axis=0, keepdims=True)

    def head_softmax_pv(hd, ki, slot, diagonal, first):
        start = pl.multiple_of(ki * tk, tk)
        for c0 in range(0, tq, QCHUNK):
            qs = slice(c0, c0 + QCHUNK)
            nk = c0 + QCHUNK if diagonal else tk
            v = v_ref[pl.ds(start, nk), hd * dh:(hd + 1) * dh]
            s = s_ref[slot, hd, 0:nk, qs]
            if diagonal:
                key = lax.broadcasted_iota(jnp.int32, (nk, QCHUNK), 0)
                qry = lax.broadcasted_iota(jnp.int32, (nk, QCHUNK), 1) + c0
                s = jnp.where(key <= qry, s, -jnp.inf)
                m_blk = jnp.max(s, axis=0, keepdims=True)
            else:
                m_blk = mb_ref[slot, hd, :, qs]
            m_new = m_blk if first else jnp.maximum(m_ref[hd, :, qs], m_blk)
            p = jnp.exp2(s - m_new)
            l_blk = jnp.sum(p, axis=0, keepdims=True)
            pv = _dot_tn(v, p.astype(BF16))
            if first:
                l_ref[hd, :, qs] = l_blk
                acc_ref[hd, :, qs] = pv
            else:
                alpha = jnp.exp2(m_ref[hd, :, qs] - m_new)
                l_ref[hd, :, qs] = alpha * l_ref[hd, :, qs] + l_blk
                acc_ref[hd, :, qs] = alpha * acc_ref[hd, :, qs] + pv
            m_ref[hd, :, qs] = m_new

    def softmax_pv(ki, slot, diagonal=False, next_block=None, first=False):
        for hd in range(nh):
            if next_block is not None:
                head_scores(hd, *next_block)
            head_softmax_pv(hd, ki, slot, diagonal, first)

    for hd in range(nh):
        head_scores(hd, qi, 0, diagonal=True)
    softmax_pv(qi, 0, diagonal=True, next_block=(0, 1), first=True)

    def pair(j, carry):
        softmax_pv(2 * j, 1, next_block=(2 * j + 1, 0))
        softmax_pv(2 * j + 1, 0, next_block=(2 * j + 2, 1))
        return carry

    lax.fori_loop(0, (qi - 1) // 2, pair, 0)

    @pl.when(qi % 2 == 1)
    def _():
        softmax_pv(qi - 1, 1)

    @pl.when((qi % 2 == 0) & (qi > 0))
    def _():
        softmax_pv(qi - 2, 1, next_block=(qi - 1, 0))
        softmax_pv(qi - 1, 0)

    for hd in range(nh):
        o_ref[rows, hd * dh:(hd + 1) * dh] = (acc_ref[hd] * (1.0 / l_ref[hd])).T.astype(o_ref.dtype)


FOX_HEADS_PER_STEP = 4
Q_BLOCKS_PER_STEP = 2
QCHUNK = 256


def _fox_attention(proj, ck_aug, *, batch, d_model):
    n = proj.shape[0]
    t = n // batch
    dh = FOX_HEAD_DIM
    nh = FOX_HEADS_PER_STEP
    groups = d_model // (nh * dh)
    assert ck_aug.shape[1] == groups * LANES
    tq = _tile(t, FOX_BLOCK)
    rows = _tile(t, Q_BLOCKS_PER_STEP * tq)
    nq = t // rows
    return pl.pallas_call(
        functools.partial(_fox_kernel, tk=tq),
        out_shape=jax.ShapeDtypeStruct((n, d_model), BF16),
        grid=(batch, groups, nq),
        in_specs=[pl.BlockSpec((rows, nh * dh), lambda b, g, i: (b * nq + i, g)),
                  pl.BlockSpec((t, nh * dh), lambda b, g, i: (b, groups + g)),
                  pl.BlockSpec((t, LANES), lambda b, g, i: (b, g)),
                  pl.BlockSpec((t, nh * dh), lambda b, g, i: (b, 2 * groups + g))],
        out_specs=pl.BlockSpec((rows, nh * dh), lambda b, g, i: (b * nq + i, g)),
        scratch_shapes=[pltpu.VMEM((2, nh, tq, tq), F32),
                        pltpu.VMEM((2, nh, 1, tq), F32),
                        pltpu.VMEM((nh, 1, tq), F32),
                        pltpu.VMEM((nh, 1, tq), F32),
                        pltpu.VMEM((nh, dh, tq), F32)],
        compiler_params=_params("parallel", "parallel", "arbitrary"),
        name="fox_attention",
    )(proj, proj, ck_aug, proj)


def _gate_weights_t(w_cols):
    g = w_cols.shape[1]
    return jnp.pad(w_cols.T, ((0, GATE_ROWS - g), (0, 0))).astype(BF16)


def _gate_bias(b):
    return jnp.pad(b.astype(F32), (0, GATE_ROWS - b.shape[0])).reshape(GATE_ROWS, 1)


def kernel(x, c, ada_w, ada_b, norm_mix_w, norm_ffn_w, mlstm_w_in, mlstm_b_gates, mlstm_norm_w, mlstm_w_out,
           fox_w_in, fox_b_f, fox_w_out, ffn_w_gate_up, ffn_w_down, final_norm_w):
    batch, t, d = x.shape
    depth = ada_w.shape[0]
    n = batch * t
    dqk = d // (2 * MLSTM_HEADS)
    mlstm_main = 2 * MLSTM_HEADS * dqk + 2 * d
    fox_main = 3 * d

    c_pad = jnp.pad(c, ((0, GATE_ROWS - batch), (0, 0)))
    mod = _modulation(c_pad, ada_w, ada_b)[:, :batch]
    mod = mod.reshape(depth, batch, 6, 1, d)

    mlstm_w_in_b, mlstm_w_out_b = mlstm_w_in.astype(BF16), mlstm_w_out.astype(BF16)
    fox_w_in_b, fox_w_out_b = fox_w_in.astype(BF16), fox_w_out.astype(BF16)
    ffn_w_gate_up_b, ffn_w_down_b = ffn_w_gate_up.astype(BF16), ffn_w_down.astype(BF16)

    xf = x.reshape(n, d)
    final_w = final_norm_w.reshape(1, d)
    for layer in range(depth):
        sh1, sc1, g1, sh2, sc2, g2 = [mod[layer, :, i] for i in range(6)]
        j = layer // 2
        nw = norm_mix_w[layer].reshape(1, d)
        if layer % 2 == 0:
            proj, gates_t = _inproj(xf, nw, sc1, sh1, mlstm_w_in_b, j, mlstm_main,
                                    _gate_weights_t(mlstm_w_in[j, :, mlstm_main:]), batch=batch,
                                    q_cols=MLSTM_HEADS * dqk, q_scale=dqk ** -0.5, gates_head_major=True)
            y = _mlstm(proj, gates_t, _gate_bias(mlstm_b_gates[j]), mlstm_norm_w[j].reshape(1, d),
                       batch=batch, d_model=d)
            xf = _outproj(y, mlstm_w_out_b, j, xf, g1, batch=batch)
        else:
            heads = d // FOX_HEAD_DIM
            wg = jnp.pad(fox_w_in[j, :, fox_main:], ((0, 0), (0, LANES - heads))).astype(BF16)
            proj, gates = _inproj(xf, nw, sc1, sh1, fox_w_in_b, j, fox_main, wg, batch=batch,
                                  q_cols=d, q_scale=FOX_HEAD_DIM ** -0.5 * LOG2E, gates_head_major=False)
            bias = jnp.pad(fox_b_f[j].astype(F32), (0, LANES - heads)).reshape(1, LANES)
            ck_aug = _cumgate(gates, bias, batch=batch, heads=heads, group=FOX_HEADS_PER_STEP)
            y = _fox_attention(proj, ck_aug, batch=batch, d_model=d)
            xf = _outproj(y, fox_w_out_b, j, xf, g1, batch=batch)
        xf = _ffn(xf, norm_ffn_w[layer].reshape(1, d), sc2, sh2, g2, ffn_w_gate_up_b, ffn_w_down_b, layer,
                  final_w, batch=batch, final_norm=(layer == depth - 1))
    return xf.reshape(batch, t, d)
```

```python
import functools

import jax
import jax.numpy as jnp
from jax import lax
from jax.experimental import pallas as pl
from jax.experimental.pallas import tpu as pltpu

F32 = jnp.float32
BF16 = jnp.bfloat16

EPS = 1e-6
GATE_SOFTCAP = 15.0
MLSTM_HEADS = 4
FOX_HEAD_DIM = 128
GATE_ROWS = 16
LANES = 128
LOG2E = 1.4426950408889634

VMEM_LIMIT = 56 * 1024 * 1024

ROW_TILE = 1024
INPROJ_COL_TILE = 2048
OUTPROJ_ROW_TILE = 512
FFN_FF_TILE = 512
MOD_COL_TILE = 1024
MLSTM_CHUNK = 256
CUM_TILE = 512
FOX_BLOCK = 512


def _params(*sem):
    return pltpu.CompilerParams(dimension_semantics=sem, vmem_limit_bytes=VMEM_LIMIT)


def _tile(n, pref):
    t = min(n, pref)
    assert n % t == 0, (n, pref)
    return t


def _log_sigmoid(x):
    return jnp.minimum(x, 0.0) - jnp.log1p(jnp.exp(-jnp.abs(x)))


def _sigmoid(x):
    return 1.0 / (1.0 + jnp.exp(-x))


def _dot(a, b):
    return jnp.dot(a, b, preferred_element_type=F32)


def _dot_nt(a, b):
    return lax.dot_general(a, b, (((1,), (1,)), ((), ())), preferred_element_type=F32)


def _dot_tn(a, b):
    return lax.dot_general(a, b, (((0,), (0,)), ((), ())), preferred_element_type=F32)


def _modulated_norm(x, nw, sc, sh):
    y = x * lax.rsqrt(jnp.mean(x * x, axis=-1, keepdims=True) + EPS)
    return (y * nw) * (1.0 + sc) + sh


def _mod_kernel(c_ref, w_ref, b_ref, o_ref):
    c = c_ref[...]
    cs = (c * _sigmoid(c)).astype(BF16)
    o_ref[0] = _dot(cs, w_ref[0].astype(BF16)) + b_ref[0]


def _modulation(c_pad, ada_w, ada_b):
    depth, d, n6 = ada_w.shape
    rows = c_pad.shape[0]
    tn = _tile(n6, MOD_COL_TILE)
    return pl.pallas_call(
        _mod_kernel,
        out_shape=jax.ShapeDtypeStruct((depth, rows, n6), F32),
        grid=(depth, n6 // tn),
        in_specs=[pl.BlockSpec((rows, d), lambda l, j: (0, 0)),
                  pl.BlockSpec((1, d, tn), lambda l, j: (l, 0, j)),
                  pl.BlockSpec((1, 1, tn), lambda l, j: (l, 0, j))],
        out_specs=pl.BlockSpec((1, rows, tn), lambda l, j: (l, 0, j)),
        compiler_params=_params("parallel", "parallel"),
        name="adaln_modulation",
    )(c_pad, ada_w, ada_b.reshape(depth, 1, n6))


def _inproj_kernel(x_ref, nw_ref, sc_ref, sh_ref, w_ref, wg_ref, o_ref, g_ref, h_ref, *, q_cols, q_scale,
                   gates_head_major):
    j = pl.program_id(1)
    tn = o_ref.shape[1]

    def project(hb):
        col = j * tn + lax.broadcasted_iota(jnp.int32, (1, tn), 1)
        scale = jnp.where(col < q_cols, q_scale, 1.0).astype(F32)
        o_ref[...] = (_dot(hb, w_ref[0]) * scale).astype(o_ref.dtype)

    @pl.when(j == 0)
    def _():
        hb = _modulated_norm(x_ref[...], nw_ref[...], sc_ref[0], sh_ref[0]).astype(BF16)
        h_ref[...] = hb
        project(hb)
        if gates_head_major:
            g_ref[0] = _dot_nt(wg_ref[...], hb)
        else:
            g_ref[...] = _dot(hb, wg_ref[...])

    @pl.when(j > 0)
    def _():
        project(h_ref[...])


def _inproj(x, nw, sc, sh, w, layer, nout, wg, *, batch, q_cols, q_scale, gates_head_major):
    n, d = x.shape
    t = n // batch
    tm = _tile(t, ROW_TILE)
    tn = _tile(nout, INPROJ_COL_TILE)
    tpb = t // tm
    kern = functools.partial(_inproj_kernel, q_cols=q_cols, q_scale=q_scale, gates_head_major=gates_head_major)
    if gates_head_major:
        g_shape = jax.ShapeDtypeStruct((batch, GATE_ROWS, t), F32)
        g_spec = pl.BlockSpec((1, GATE_ROWS, tm), lambda i, j: (i // tpb, 0, i % tpb))
    else:
        g_shape = jax.ShapeDtypeStruct((n, LANES), F32)
        g_spec = pl.BlockSpec((tm, LANES), lambda i, j: (i, 0))
    return pl.pallas_call(
        kern,
        out_shape=(jax.ShapeDtypeStruct((n, nout), BF16), g_shape),
        grid=(n // tm, nout // tn),
        in_specs=[pl.BlockSpec((tm, d), lambda i, j: (i, 0)),
                  pl.BlockSpec((1, d), lambda i, j: (0, 0)),
                  pl.BlockSpec((1, 1, d), lambda i, j: (i // tpb, 0, 0)),
                  pl.BlockSpec((1, 1, d), lambda i, j: (i // tpb, 0, 0)),
                  pl.BlockSpec((1, d, tn), lambda i, j: (layer, 0, j)),
                  pl.BlockSpec(wg.shape, lambda i, j: (0, 0))],
        out_specs=(pl.BlockSpec((tm, tn), lambda i, j: (i, j)), g_spec),
        scratch_shapes=[pltpu.VMEM((tm, d), BF16)],
        compiler_params=_params("parallel", "arbitrary"),
        name="norm_inproj",
    )(x, nw, sc, sh, w, wg)


def _outproj_kernel(y_ref, w_ref, x_ref, g_ref, o_ref, *, y_feature_major):
    yw = _dot_tn(y_ref[0], w_ref[0]) if y_feature_major else _dot(y_ref[...], w_ref[0])
    o_ref[...] = x_ref[...] + g_ref[0] * yw


def _outproj(y, w, layer, x, gate, *, batch, y_feature_major=False):
    n = x.shape[0]
    kdim = w.shape[1]
    d = w.shape[2]
    t = n // batch
    tm = _tile(t, OUTPROJ_ROW_TILE)
    tn = d
    tpb = t // tm
    if y_feature_major:
        y_spec = pl.BlockSpec((1, kdim, tm), lambda i, j: (i // tpb, 0, i % tpb))
    else:
        y_spec = pl.BlockSpec((tm, kdim), lambda i, j: (i, 0))
    return pl.pallas_call(
        functools.partial(_outproj_kernel, y_feature_major=y_feature_major),
        out_shape=jax.ShapeDtypeStruct((n, d), F32),
        grid=(n // tm, d // tn),
        in_specs=[y_spec,
                  pl.BlockSpec((1, kdim, tn), lambda i, j: (layer, 0, j)),
                  pl.BlockSpec((tm, tn), lambda i, j: (i, j)),
                  pl.BlockSpec((1, 1, tn), lambda i, j: (i // tpb, 0, j))],
        out_specs=pl.BlockSpec((tm, tn), lambda i, j: (i, j)),
        compiler_params=_params("parallel", "arbitrary"),
        name="outproj_residual",
    )(y, w, x, gate)


def _ffn_kernel(x_ref, nw_ref, sc_ref, sh_ref, g2_ref, wg_ref, wu_ref, wd_ref, fw_ref, o_ref, h_ref, *, final_norm):
    f = pl.program_id(1)

    def contribution(h):
        g = _dot(h, wg_ref[0])
        u = _dot(h, wu_ref[0])
        a = (g * _sigmoid(g) * u).astype(BF16)
        return g2_ref[0] * _dot(a, wd_ref[0])

    @pl.when(f == 0)
    def _():
        x = x_ref[...]
        hb = _modulated_norm(x, nw_ref[...], sc_ref[0], sh_ref[0]).astype(BF16)
        h_ref[...] = hb
        o_ref[...] = x + contribution(hb)

    @pl.when(f > 0)
    def _():
        o_ref[...] += contribution(h_ref[...])

    if final_norm:
        @pl.when(f == pl.num_programs(1) - 1)
        def _():
            out = o_ref[...]
            o_ref[...] = out * lax.rsqrt(jnp.mean(out * out, axis=-1, keepdims=True) + EPS) * fw_ref[...]


def _ffn(x, nw, sc, sh, g2, w_gate_up, w_down, layer, final_w, *, batch, final_norm):
    n, d = x.shape
    dff = w_down.shape[1]
    t = n // batch
    tm = _tile(t, ROW_TILE)
    tf = _tile(dff, FFN_FF_TILE)
    tpb = t // tm
    nf = dff // tf
    vec = lambda: pl.BlockSpec((1, 1, d), lambda i, f: (i // tpb, 0, 0))
    return pl.pallas_call(
        functools.partial(_ffn_kernel, final_norm=final_norm),
        out_shape=jax.ShapeDtypeStruct((n, d), F32),
        grid=(n // tm, nf),
        in_specs=[pl.BlockSpec((tm, d), lambda i, f: (i, 0)),
                  pl.BlockSpec((1, d), lambda i, f: (0, 0)),
                  vec(), vec(), vec(),
                  pl.BlockSpec((1, d, tf), lambda i, f: (layer, 0, f)),
                  pl.BlockSpec((1, d, tf), lambda i, f: (layer, 0, nf + f)),
                  pl.BlockSpec((1, tf, d), lambda i, f: (layer, f, 0)),
                  pl.BlockSpec((1, d), lambda i, f: (0, 0))],
        out_specs=pl.BlockSpec((tm, d), lambda i, f: (i, 0)),
        scratch_shapes=[pltpu.VMEM((tm, d), BF16)],
        compiler_params=_params("parallel", "arbitrary"),
        name="swiglu_ffn",
    )(x, nw, sc, sh, g2, w_gate_up, w_gate_up, w_down, final_w)


def _mlstm_kernel(q_ref, k_ref, v_ref, o_ref, g_ref, bias_ref, nw_ref, y_ref, c_ref, n_ref, m_ref, *, heads):
    ci = pl.program_id(1)
    L = q_ref.shape[0]
    dqk = q_ref.shape[1] // heads
    dv = v_ref.shape[1] // heads

    @pl.when(ci == 0)
    def _():
        c_ref[...] = jnp.zeros_like(c_ref)
        n_ref[...] = jnp.zeros_like(n_ref)
        m_ref[...] = jnp.zeros_like(m_ref)

    ones = jnp.ones((L, LANES), BF16)
    ones_dv = jnp.ones((dv, LANES), BF16)
    q = [q_ref[:, h * dqk:(h + 1) * dqk] for h in range(heads)]
    k = [k_ref[:, h * dqk:(h + 1) * dqk] for h in range(heads)]
    qk = [_dot_nt(q[h], k[h]) for h in range(heads)]
    qc = [_dot(q[h], c_ref[h].astype(BF16)) for h in range(heads)]
    qn = [_dot(q[h], n_ref[h].astype(BF16)) for h in range(heads)]

    gates = g_ref[0] + bias_ref[...]
    gates = GATE_SOFTCAP * jnp.tanh(gates / GATE_SOFTCAP)
    row = lax.broadcasted_iota(jnp.int32, (L, L), 0)
    col = lax.broadcasted_iota(jnp.int32, (L, L), 1)
    causal = row >= col
    eye = row == col

    for h in range(heads):
        li_row = gates[h:h + 1, :]
        lf_row = _log_sigmoid(gates[heads + h:heads + h + 1, :])
        b_col = jnp.sum(jnp.where(causal, lf_row, 0.0), axis=1, keepdims=True)
        b_row = jnp.sum(jnp.where(eye, b_col, 0.0), axis=0, keepdims=True)
        li_col = jnp.sum(jnp.where(eye, li_row, 0.0), axis=1, keepdims=True)
        g_tot = jnp.sum(lf_row, axis=1, keepdims=True)

        m_prev = m_ref[h:h + 1, 0:1]
        dmat = jnp.where(causal, b_col + (li_row - b_row), -jnp.inf)
        inter_log = b_col + m_prev
        m_t = jnp.maximum(inter_log, jnp.max(dmat, axis=1, keepdims=True))
        w_intra = jnp.exp(dmat - m_t)
        w_inter = jnp.exp(inter_log - m_t)

        v = v_ref[:, h * dv:(h + 1) * dv]
        s = (qk[h] * w_intra).astype(BF16)
        num = w_inter * qc[h] + _dot(s, v)
        den = w_inter * qn[h][:, 0:1] + _dot(s, ones)[:, 0:1]
        r = 1.0 / jnp.maximum(jnp.abs(den), jnp.exp(-m_t))
        sumsq = _dot((num * num).astype(BF16), ones_dv)[:, 0:1]
        rms_h = r * jnp.sqrt(sumsq * (1.0 / dv))
        row_scale = r * lax.rsqrt(rms_h * rms_h + EPS)

        tail = g_tot - b_col + li_col
        m_new = jnp.maximum(g_tot + m_prev, jnp.max(tail, axis=0, keepdims=True))
        w_tail = jnp.exp(tail - m_new)
        decay = jnp.exp(g_tot + m_prev - m_new)
        kw = (k[h].astype(F32) * w_tail).astype(BF16)
        c_ref[h] = decay * c_ref[h] + _dot_tn(kw, v)
        n_ref[h] = decay * n_ref[h] + _dot_tn(kw, ones)
        m_ref[h:h + 1, :] = jnp.broadcast_to(m_new, (1, LANES))

        half_hs = (num * (0.5 * row_scale) * nw_ref[:, h * dv:(h + 1) * dv]).astype(BF16)
        t = jnp.tanh(o_ref[:, h * dv:(h + 1) * dv] * 0.5)
        y_ref[:, h * dv:(h + 1) * dv] = (half_hs + half_hs * t).astype(y_ref.dtype)


def _mlstm(proj, gates_t, bias, norm_w, *, batch, d_model):
    n = proj.shape[0]
    t = n // batch
    heads = MLSTM_HEADS
    dqk = d_model // (2 * heads)
    dv = d_model // heads
    L = _tile(t, MLSTM_CHUNK)
    nc = t // L
    qk_w = heads * dqk
    assert (2 * qk_w) % (heads * dv) == 0 and heads <= 8
    v_blk = (2 * qk_w) // (heads * dv)
    return pl.pallas_call(
        functools.partial(_mlstm_kernel, heads=heads),
        out_shape=jax.ShapeDtypeStruct((n, heads * dv), BF16),
        grid=(batch, nc),
        in_specs=[pl.BlockSpec((L, qk_w), lambda b, c: (b * nc + c, 0)),
                  pl.BlockSpec((L, qk_w), lambda b, c: (b * nc + c, 1)),
                  pl.BlockSpec((L, heads * dv), lambda b, c: (b * nc + c, v_blk)),
                  pl.BlockSpec((L, heads * dv), lambda b, c: (b * nc + c, v_blk + 1)),
                  pl.BlockSpec((1, GATE_ROWS, L), lambda b, c: (b, 0, c)),
                  pl.BlockSpec((GATE_ROWS, 1), lambda b, c: (0, 0)),
                  pl.BlockSpec((1, heads * dv), lambda b, c: (0, 0))],
        out_specs=pl.BlockSpec((L, heads * dv), lambda b, c: (b * nc + c, 0)),
        scratch_shapes=[pltpu.VMEM((heads, dqk, dv), F32), pltpu.VMEM((heads, dqk, LANES), F32),
                        pltpu.VMEM((8, LANES), F32)],
        compiler_params=_params("parallel", "arbitrary"),
        name="mlstm_chunkwise",
    )(proj, proj, proj, proj, gates_t, bias, norm_w)


BIAS_PIECES = 3


def _split_bf16(x):
    p0 = x.astype(BF16)
    r1 = x - p0.astype(F32)
    p1 = r1.astype(BF16)
    p2 = (r1 - p1.astype(F32)).astype(BF16)
    return p0, p1, p2


def _cumgate_kernel(g_ref, bias_ref, o_ref, carry_ref, *, group):
    @pl.when(pl.program_id(1) == 0)
    def _():
        carry_ref[...] = jnp.zeros_like(carry_ref)

    lf = _log_sigmoid(g_ref[...] + bias_ref[...])
    tc = lf.shape[0]
    row = lax.broadcasted_iota(jnp.int32, (tc, tc), 0)
    col = lax.broadcasted_iota(jnp.int32, (tc, tc), 1)
    lower = jnp.where(row >= col, 1.0, 0.0).astype(BF16)
    cum = carry_ref[0:1, :]
    for piece in _split_bf16(lf):
        cum = cum + _dot(lower, piece)
    carry_ref[...] = jnp.broadcast_to(cum[tc - 1:tc, :], carry_ref.shape)

    width = o_ref.shape[1]
    src = lax.broadcasted_iota(jnp.int32, (LANES, width), 0)
    dst = lax.broadcasted_iota(jnp.int32, (LANES, width), 1)
    dst_head = (dst // LANES) * group + dst % group
    dst_piece = (dst % LANES) // group
    out = jnp.zeros((tc, width), F32)
    for j, piece in enumerate(_split_bf16(cum * LOG2E)):
        sel = jnp.where((dst_head == src) & (dst_piece == j), 1.0, 0.0).astype(BF16)
        out = out + _dot(piece, sel)
    o_ref[...] = out.astype(o_ref.dtype)


def _cumgate(gates, bias, *, batch, heads, group):
    n, lanes = gates.shape
    t = n // batch
    tc = _tile(t, CUM_TILE)
    nc = t // tc
    assert heads % group == 0 and group & (group - 1) == 0 and BIAS_PIECES * group <= LANES
    width = (heads // group) * LANES
    return pl.pallas_call(
        functools.partial(_cumgate_kernel, group=group),
        out_shape=jax.ShapeDtypeStruct((n, width), BF16),
        grid=(batch, nc),
        in_specs=[pl.BlockSpec((tc, lanes), lambda b, c: (b * nc + c, 0)),
                  pl.BlockSpec((1, lanes), lambda b, c: (0, 0))],
        out_specs=pl.BlockSpec((tc, width), lambda b, c: (b * nc + c, 0)),
        scratch_shapes=[pltpu.VMEM((8, lanes), F32)],
        compiler_params=_params("parallel", "arbitrary"),
        name="fox_cum_log_forget",
    )(gates, bias)


def _fox_kernel(q_ref, k_ref, ck_ref, v_ref, o_ref, *scratch, tk):
    per_step = q_ref.shape[0] // tk
    for sub in range(per_step):
        _fox_query_block(pl.program_id(2) * per_step + sub, slice(sub * tk, (sub + 1) * tk),
                         q_ref, k_ref, ck_ref, v_ref, o_ref, *scratch, tk=tk)


def _fox_query_block(qi, rows, q_ref, k_ref, ck_ref, v_ref, o_ref, s_ref, mb_ref, m_ref, l_ref, acc_ref, *, tk):
    tq = tk
    dh = FOX_HEAD_DIM
    nh = q_ref.shape[1] // dh
    lane = lax.broadcasted_iota(jnp.int32, (tq, LANES), 1)
    q_aug = []
    for hd in range(nh):
        minus_one = jnp.where((lane % nh == hd) & (lane < BIAS_PIECES * nh), -1.0, 0.0).astype(BF16)
        q_aug.append(jnp.concatenate([q_ref[rows, hd * dh:(hd + 1) * dh], minus_one], axis=1))

    def head_scores(hd, ki, slot, diagonal=False):
        start = pl.multiple_of(ki * tk, tk)
        k_aug = jnp.concatenate([k_ref[pl.ds(start, tk), hd * dh:(hd + 1) * dh],
                                 ck_ref[pl.ds(start, tk), :]], axis=1)
        if diagonal:
            for c0 in range(0, tq, QCHUNK):
                s_ref[slot, hd, 0:c0 + QCHUNK, c0:c0 + QCHUNK] = _dot_nt(k_aug[0:c0 + QCHUNK],
                                                                           q_aug[hd][c0:c0 + QCHUNK])
        else:
            for c0 in range(0, tq, QCHUNK):
                qs = slice(c0, c0 + QCHUNK)
                s = _dot_nt(k_aug, q_aug[hd][qs])
                s_ref[slot, hd, :, qs] = s
                mb_ref[slot, hd, :, qs] = jnp.max(s, axis=0, keepdims=True)

    def head_softmax_pv(hd, ki, slot, diagonal, first):
        start = pl.multiple_of(ki * tk, tk)
        for c0 in range(0, tq, QCHUNK):
            qs = slice(c0, c0 + QCHUNK)
            nk = c0 + QCHUNK if diagonal else tk
            v = v_ref[pl.ds(start, nk), hd * dh:(hd + 1) * dh]
            s = s_ref[slot, hd, 0:nk, qs]
            if diagonal:
                key = lax.broadcasted_iota(jnp.int32, (nk, QCHUNK), 0)
                qry = lax.broadcasted_iota(jnp.int32, (nk, QCHUNK), 1) + c0
                s = jnp.where(key <= qry, s, -jnp.inf)
                m_blk = jnp.max(s, axis=0, keepdims=True)
            else:
                m_blk = mb_ref[slot, hd, :, qs]
            m_new = m_blk if first else jnp.maximum(m_ref[hd, :, qs], m_blk)
            p = jnp.exp2(s - m_new)
            l_blk = jnp.sum(p, axis=0, keepdims=True)
            pv = _dot_tn(v, p.astype(BF16))
            if first:
                l_ref[hd, :, qs] = l_blk
                acc_ref[hd, :, qs] = pv
            else:
                alpha = jnp.exp2(m_ref[hd, :, qs] - m_new)
                l_ref[hd, :, qs] = alpha * l_ref[hd, :, qs] + l_blk
                acc_ref[hd, :, qs] = alpha * acc_ref[hd, :, qs] + pv
            m_ref[hd, :, qs] = m_new

    def softmax_pv(ki, slot, diagonal=False, next_block=None, first=False):
        for hd in range(nh):
            if next_block is not None:
                head_scores(hd, *next_block)
            head_softmax_pv(hd, ki, slot, diagonal, first)

    for hd in range(nh):
        head_scores(hd, qi, 0, diagonal=True)
    softmax_pv(qi, 0, diagonal=True, next_block=(0, 1), first=True)

    def pair(j, carry):
        softmax_pv(2 * j, 1, next_block=(2 * j + 1, 0))
        softmax_pv(2 * j + 1, 0, next_block=(2 * j + 2, 1))
        return carry

    lax.fori_loop(0, (qi - 1) // 2, pair, 0)

    @pl.when(qi % 2 == 1)
    def _():
        softmax_pv(qi - 1, 1)

    @pl.when((qi % 2 == 0) & (qi > 0))
    def _():
        softmax_pv(qi - 2, 1, next_block=(qi - 1, 0))
        softmax_pv(qi - 1, 0)

    for hd in range(nh):
        o_ref[0, hd * dh:(hd + 1) * dh, rows] = (acc_ref[hd] * (1.0 / l_ref[hd])).astype(o_ref.dtype)


FOX_HEADS_PER_STEP = 4
Q_BLOCKS_PER_STEP = 2
QCHUNK = 256


def _fox_attention(proj, ck_aug, *, batch, d_model):
    n = proj.shape[0]
    t = n // batch
    dh = FOX_HEAD_DIM
    nh = FOX_HEADS_PER_STEP
    groups = d_model // (nh * dh)
    assert ck_aug.shape[1] == groups * LANES
    tq = _tile(t, FOX_BLOCK)
    rows = _tile(t, Q_BLOCKS_PER_STEP * tq)
    nq = t // rows
    return pl.pallas_call(
        functools.partial(_fox_kernel, tk=tq),
        out_shape=jax.ShapeDtypeStruct((batch, d_model, t), BF16),
        grid=(batch, groups, nq),
        in_specs=[pl.BlockSpec((rows, nh * dh), lambda b, g, i: (b * nq + i, g)),
                  pl.BlockSpec((t, nh * dh), lambda b, g, i: (b, groups + g)),
                  pl.BlockSpec((t, LANES), lambda b, g, i: (b, g)),
                  pl.BlockSpec((t, nh * dh), lambda b, g, i: (b, 2 * groups + g))],
        out_specs=pl.BlockSpec((1, nh * dh, rows), lambda b, g, i: (b, g, i)),
        scratch_shapes=[pltpu.VMEM((2, nh, tq, tq), F32),
                        pltpu.VMEM((2, nh, 1, tq), F32),
                        pltpu.VMEM((nh, 1, tq), F32),
                        pltpu.VMEM((nh, 1, tq), F32),
                        pltpu.VMEM((nh, dh, tq), F32)],
        compiler_params=_params("parallel", "parallel", "arbitrary"),
        name="fox_attention",
    )(proj, proj, ck_aug, proj)


def _gate_weights_t(w_cols):
    g = w_cols.shape[1]
    return jnp.pad(w_cols.T, ((0, GATE_ROWS - g), (0, 0))).astype(BF16)


def _gate_bias(b):
    return jnp.pad(b.astype(F32), (0, GATE_ROWS - b.shape[0])).reshape(GATE_ROWS, 1)


def kernel(x, c, ada_w, ada_b, norm_mix_w, norm_ffn_w, mlstm_w_in, mlstm_b_gates, mlstm_norm_w, mlstm_w_out,
           fox_w_in, fox_b_f, fox_w_out, ffn_w_gate_up, ffn_w_down, final_norm_w):
    batch, t, d = x.shape
    depth = ada_w.shape[0]
    n = batch * t
    dqk = d // (2 * MLSTM_HEADS)
    mlstm_main = 2 * MLSTM_HEADS * dqk + 2 * d
    fox_main = 3 * d

    c_pad = jnp.pad(c, ((0, GATE_ROWS - batch), (0, 0)))
    mod = _modulation(c_pad, ada_w, ada_b)[:, :batch]
    mod = mod.reshape(depth, batch, 6, 1, d)

    mlstm_w_in_b, mlstm_w_out_b = mlstm_w_in.astype(BF16), mlstm_w_out.astype(BF16)
    fox_w_in_b, fox_w_out_b = fox_w_in.astype(BF16), fox_w_out.astype(BF16)
    ffn_w_gate_up_b, ffn_w_down_b = ffn_w_gate_up.astype(BF16), ffn_w_down.astype(BF16)

    xf = x.reshape(n, d)
    final_w = final_norm_w.reshape(1, d)
    for layer in range(depth):
        sh1, sc1, g1, sh2, sc2, g2 = [mod[layer, :, i] for i in range(6)]
        j = layer // 2
        nw = norm_mix_w[layer].reshape(1, d)
        if layer % 2 == 0:
            proj, gates_t = _inproj(xf, nw, sc1, sh1, mlstm_w_in_b, j, mlstm_main,
                                    _gate_weights_t(mlstm_w_in[j, :, mlstm_main:]), batch=batch,
                                    q_cols=MLSTM_HEADS * dqk, q_scale=dqk ** -0.5, gates_head_major=True)
            y = _mlstm(proj, gates_t, _gate_bias(mlstm_b_gates[j]), mlstm_norm_w[j].reshape(1, d),
                       batch=batch, d_model=d)
            xf = _outproj(y, mlstm_w_out_b, j, xf, g1, batch=batch)
        else:
            heads = d // FOX_HEAD_DIM
            wg = jnp.pad(fox_w_in[j, :, fox_main:], ((0, 0), (0, LANES - heads))).astype(BF16)
            proj, gates = _inproj(xf, nw, sc1, sh1, fox_w_in_b, j, fox_main, wg, batch=batch,
                                  q_cols=d, q_scale=FOX_HEAD_DIM ** -0.5 * LOG2E, gates_head_major=False)
            bias = jnp.pad(fox_b_f[j].astype(F32), (0, LANES - heads)).reshape(1, LANES)
            ck_aug = _cumgate(gates, bias, batch=batch, heads=heads, group=FOX_HEADS_PER_STEP)
            y = _fox_attention(proj, ck_aug, batch=batch, d_model=d)
            xf = _outproj(y, fox_w_out_b, j, xf, g1, batch=batch, y_feature_major=True)
        xf = _ffn(xf, norm_ffn_w[layer].reshape(1, d), sc2, sh2, g2, ffn_w_gate_up_b, ffn_w_down_b, layer,
                  final_w, batch=batch, final_norm=(layer == depth - 1))
    return xf.reshape(batch, t, d)
```
